```python
import math
import functools
import jax, jax.numpy as jnp
from jax import lax
import numpy as np

D_MODEL = 1024
BATCH = 4
SEQ = 8192
DEPTH = 1
DEC_BATCH = 128
DEC_SEQ = 8
PAST_LEN = 8192
PAGE_SIZE = 128

FOX_HEAD_DIM = 64
FOX_HEADS = D_MODEL // 128
FOX_BLOCK = 128
FOX_FORGET_BIAS = 3.0
GLA_HEADS = 4
GLA_DK = D_MODEL // 2 // GLA_HEADS
GLA_DV = D_MODEL // GLA_HEADS
GLA_GATE_RANK = 16
GLA_TAU = 16.0
GLA_CHUNK = 64
N_EXPERTS = 32
TOP_K = 4
D_FF = D_MODEL
SWIGLU_ALPHA = 1.702
SWIGLU_LIMIT = 7.0
MOE_BLOCK = 128
PLE_DIM = 256
DEEPNORM_ALPHA = (2.0 * DEPTH) ** 0.25
DEEPNORM_BETA = (8.0 * DEPTH) ** -0.25
LN_EPS = 1e-5
RMS_EPS = 1e-6
IN_SPLITS = (FOX_HEADS * FOX_HEAD_DIM, FOX_HEADS * FOX_HEAD_DIM, FOX_HEADS * FOX_HEAD_DIM, FOX_HEADS,
             GLA_HEADS * GLA_DK, GLA_HEADS * GLA_DK, GLA_HEADS * GLA_DV, GLA_HEADS * GLA_DV, GLA_GATE_RANK,
             D_MODEL, D_MODEL)
IN_WIDTH = sum(IN_SPLITS)
IN_SPLIT_IDX = tuple(np.cumsum(IN_SPLITS)[:-1].tolist())

kernel_name = 'fox_gla_moe_hybrid_step'


def layer_norm(x, g, b):
    xf = x.astype(jnp.float32)
    mu = jnp.mean(xf, axis=-1, keepdims=True)
    var = jnp.mean(jnp.square(xf - mu), axis=-1, keepdims=True)
    return (xf - mu) * lax.rsqrt(var + LN_EPS) * g + b


def mixer_inputs(x, w_in, b_f, w_a_up, b_a):
    n, l, _ = x.shape
    fq, fk, fv, ff, gq, gk, gv, gg, glr, ga, gb = jnp.split(x @ w_in, IN_SPLIT_IDX, axis=-1)
    fox_shape = (n, l, FOX_HEADS, FOX_HEAD_DIM)
    logf = jax.nn.log_sigmoid((ff + b_f).astype(jnp.float32))
    log_a = jax.nn.log_sigmoid((glr @ w_a_up + b_a).astype(jnp.float32)) / GLA_TAU
    return (fq.reshape(fox_shape), fk.reshape(fox_shape), fv.reshape(fox_shape), logf,
            gq.reshape(n, l, GLA_HEADS, GLA_DK) * GLA_DK ** -0.5,
            gk.reshape(n, l, GLA_HEADS, GLA_DK),
            gv.reshape(n, l, GLA_HEADS, GLA_DV),
            log_a.reshape(n, l, GLA_HEADS, GLA_DK),
            jax.nn.silu(gg), jax.nn.sigmoid(ga), jax.nn.sigmoid(gb))


def fox_prompt(q, k, v, logf):
    n, s_len, h, dh = q.shape
    nb = s_len // FOX_BLOCK
    c = jnp.cumsum(logf, axis=1)
    c_keys = c.transpose(0, 2, 1)[:, :, None, :]
    kf, vf = k.astype(jnp.float32), v.astype(jnp.float32)
    qb = (q.astype(jnp.float32) * dh ** -0.5).reshape(n, nb, FOX_BLOCK, h, dh).swapaxes(0, 1)
    cb = c.reshape(n, nb, FOX_BLOCK, h).swapaxes(0, 1)
    key_pos = jnp.arange(s_len)

    def block(args):
        i, q_i, c_i = args
        s = jnp.einsum('nqhd,nkhd->nhqk', q_i, kf) + c_i.transpose(0, 2, 1)[..., None] - c_keys
        q_pos = i * FOX_BLOCK + jnp.arange(FOX_BLOCK)
        s = jnp.where(key_pos[None, :] <= q_pos[:, None], s, -jnp.inf)
        return jnp.einsum('nhqk,nkhd->nqhd', jax.nn.softmax(s, axis=-1), vf)

    out = lax.map(block, (jnp.arange(nb), qb, cb))
    return out.swapaxes(0, 1).reshape(n, s_len, h, dh)


def fox_sample(q, k, v, logf, cache_k, cache_v, cache_logf, page_table, layer):
    n, l, h, dh = q.shape
    n_pages = page_table.shape[1]
    page = cache_k.shape[2]
    past_logf = cache_logf[layer][page_table].astype(jnp.float32).reshape(n, n_pages * page, h)
    c_past = jnp.cumsum(past_logf, axis=1)
    c_new = c_past[:, -1:, :] + jnp.cumsum(logf, axis=1)
    c_q = c_new.transpose(0, 2, 1)[..., None]
    c_pages = c_past.reshape(n, n_pages, page, h).transpose(1, 0, 3, 2)
    qf = q.astype(jnp.float32) * dh ** -0.5

    def accumulate(carry, s, vals):
        m, den, acc = carry
        m_new = jnp.maximum(m, jnp.max(s, axis=-1))
        corr = jnp.exp(m - m_new)
        p = jnp.exp(s - m_new[..., None])
        return (m_new, den * corr + jnp.sum(p, axis=-1),
                acc * corr[..., None] + jnp.einsum('nhqk,nkhd->nhqd', p, vals))

    def page_step(carry, xs):
        pt, c_pg = xs
        kp = cache_k[layer, pt].astype(jnp.float32)
        vp = cache_v[layer, pt].astype(jnp.float32)
        s = jnp.einsum('nqhd,nkhd->nhqk', qf, kp) + c_q - c_pg[:, :, None, :]
        return accumulate(carry, s, vp), None

    init = (jnp.full((n, h, l), -1e30, jnp.float32), jnp.zeros((n, h, l), jnp.float32),
            jnp.zeros((n, h, l, dh), jnp.float32))
    carry, _ = lax.scan(page_step, init, (page_table.T, c_pages))
    s_new = jnp.einsum('nqhd,nkhd->nhqk', qf, k.astype(jnp.float32)) + c_q - c_new.transpose(0, 2, 1)[:, :, None, :]
    s_new = jnp.where(jnp.tril(jnp.ones((l, l), bool)), s_new, -jnp.inf)
    m, den, acc = accumulate(carry, s_new, v.astype(jnp.float32))
    return (acc / den[..., None]).transpose(0, 2, 1, 3)


def gla_chunked(q, k, v, log_a, s0):
    n, l, h, dk = q.shape
    dv = v.shape[-1]
    c = math.gcd(l, GLA_CHUNK)
    nc = l // c

    def chunks(t):
        return t.astype(jnp.float32).reshape(n, nc, c, *t.shape[2:]).swapaxes(0, 1)

    qc, kc, vc = chunks(q), chunks(k), chunks(v)
    b = jnp.cumsum(chunks(log_a), axis=2)
    b_last = b[:, :, -1:]
    q_dec = qc * jnp.exp(b)
    k_end = kc * jnp.exp(b_last - b)
    a_tot = jnp.exp(b_last[:, :, 0])
    attn = jnp.einsum('cnthd,cnshd->cnhts', q_dec, kc * jnp.exp(-b))
    attn = jnp.where(jnp.tril(jnp.ones((c, c), bool)), attn, 0.0)
    o_intra = jnp.einsum('cnhts,cnshv->cnthv', attn, vc)

    def step(state, xs):
        q_i, k_i, v_i, a_i = xs
        o_i = jnp.einsum('nthd,nhdv->nthv', q_i, state)
        state = state * a_i[..., None] + jnp.einsum('nshd,nshv->nhdv', k_i, v_i)
        return state, o_i

    s_fin, o_inter = lax.scan(step, s0.astype(jnp.float32), (q_dec, k_end, vc, a_tot))
    return (o_inter + o_intra).swapaxes(0, 1).reshape(n, l, h, dv), s_fin


def mixer_output(o_fox, o_gla, g_out, gate_a, gate_b, gla_norm, w_o_fox, w_o_gla, w_out):
    n, l = o_fox.shape[:2]
    o_gla = o_gla * lax.rsqrt(jnp.mean(jnp.square(o_gla), axis=-1, keepdims=True) + RMS_EPS) * gla_norm
    branch_a = o_fox.reshape(n, l, -1) @ w_o_fox
    branch_b = (o_gla.reshape(n, l, -1) * g_out) @ w_o_gla
    return (gate_a * branch_a + gate_b * branch_b) @ w_out


def moe(x, w_router, b_router, w_gu, b_gu, w_down, b_down):
    n, l, dm = x.shape
    t = n * l
    xt = x.reshape(t, dm)
    logits = (xt @ w_router + b_router).astype(jnp.float32)
    top_val, top_idx = lax.top_k(logits, TOP_K)
    top_w = jax.nn.softmax(top_val, axis=-1)
    flat_e = top_idx.reshape(-1)
    flat_tok = jnp.repeat(jnp.arange(t, dtype=jnp.int32), TOP_K)
    flat_w = top_w.reshape(-1)
    order = jnp.argsort(flat_e)
    se, stok, sw = flat_e[order], flat_tok[order], flat_w[order]
    counts = jnp.bincount(flat_e, length=N_EXPERTS)
    starts = jnp.cumsum(counts) - counts
    pcounts = (counts + MOE_BLOCK - 1) // MOE_BLOCK * MOE_BLOCK
    pends = jnp.cumsum(pcounts)
    pstarts = pends - pcounts
    dest = pstarts[se] + jnp.arange(t * TOP_K) - starts[se]
    n_blocks = (t * TOP_K + MOE_BLOCK - 1) // MOE_BLOCK + N_EXPERTS
    n_slots = n_blocks * MOE_BLOCK
    slot_tok = jnp.full((n_slots,), t, jnp.int32).at[dest].set(stok)
    slot_w = jnp.zeros((n_slots,), jnp.float32).at[dest].set(sw)
    block_e = jnp.minimum(jnp.searchsorted(pends, jnp.arange(n_blocks) * MOE_BLOCK, side='right'), N_EXPERTS - 1)
    x_pad = jnp.concatenate([xt, jnp.zeros((1, dm), xt.dtype)], axis=0)
    xb = x_pad[slot_tok].reshape(n_blocks, MOE_BLOCK, dm)

    def expert_block(args):
        xe, e = args
        gate, up = jnp.split(xe @ w_gu[e] + b_gu[e], 2, axis=-1)
        gate = jnp.minimum(gate, SWIGLU_LIMIT)
        up = jnp.clip(up, -SWIGLU_LIMIT, SWIGLU_LIMIT)
        hid = (up + 1.0) * gate * jax.nn.sigmoid(SWIGLU_ALPHA * gate)
        return hid @ w_down[e] + b_down[e]

    yb = lax.map(expert_block, (xb, block_e)).reshape(n_slots, dm)
    y = jnp.zeros((t + 1, dm), jnp.float32).at[slot_tok].add(yb * slot_w[:, None])
    return y[:t].reshape(n, l, dm)


def decoder_layer(x, p, attn_fn, s0, w_in, b_f, w_a_up, b_a, gla_norm, w_o_fox, w_o_gla, w_out,
                  ln1_g, ln1_b, w_router, b_router, w_gu, b_gu, w_down, b_down,
                  w_ple_gate, b_ple_gate, w_ple_proj, ln2_g, ln2_b):
    fq, fk, fv, logf, gq, gk, gv, log_a, g_out, gate_a, gate_b = mixer_inputs(x, w_in, b_f, w_a_up, b_a)
    o_fox = attn_fn(fq, fk, fv, logf)
    o_gla, s_fin = gla_chunked(gq, gk, gv, log_a, s0)
    mix = mixer_output(o_fox, o_gla, g_out, gate_a, gate_b, gla_norm, w_o_fox, w_o_gla, w_out)
    x1 = layer_norm(DEEPNORM_ALPHA * x + mix, ln1_g, ln1_b)
    ple = jax.nn.sigmoid(x1 @ w_ple_gate + b_ple_gate) * (p @ w_ple_proj)
    ch = moe(x1, w_router, b_router, w_gu, b_gu, w_down, b_down) + ple
    x2 = layer_norm(DEEPNORM_ALPHA * x1 + ch, ln2_g, ln2_b)
    return x2, (fk, fv, logf, s_fin)


def setup_inputs(seed: int = 0) -> dict:
    key = jax.random.key(seed)
    ks = jax.random.split(key, 32)

    def nrm(k, shape, scale):
        return jax.random.normal(k, shape, jnp.float32) * scale

    n_pages = PAST_LEN // PAGE_SIZE
    n_used = DEC_BATCH * n_pages
    n_pool = n_used + max(1, n_used // 4)
    page_table = jax.random.permutation(ks[0], n_pool)[:n_used].reshape(DEC_BATCH, n_pages).astype(jnp.int32)
    fox_w = FOX_HEADS * FOX_HEAD_DIM
    gla_v = GLA_HEADS * GLA_DV
    return {
        'x_prompt': nrm(ks[1], (BATCH, SEQ, D_MODEL), 1.0),
        'x_sample': nrm(ks[2], (DEC_BATCH, DEC_SEQ, D_MODEL), 1.0),
        'cache_k': nrm(ks[3], (DEPTH, n_pool, PAGE_SIZE, FOX_HEADS, FOX_HEAD_DIM), 1.0),
        'cache_v': nrm(ks[4], (DEPTH, n_pool, PAGE_SIZE, FOX_HEADS, FOX_HEAD_DIM), 1.0),
        'cache_logf': jax.nn.log_sigmoid(FOX_FORGET_BIAS + nrm(ks[5], (DEPTH, n_pool, PAGE_SIZE, FOX_HEADS), 1.0)),
        'state_gla': nrm(ks[6], (DEPTH, DEC_BATCH, GLA_HEADS, GLA_DK, GLA_DV), 1.0),
        'page_table': page_table,
        'p_prompt': nrm(ks[7], (DEPTH, BATCH, SEQ, PLE_DIM), 1.0),
        'p_sample': nrm(ks[8], (DEPTH, DEC_BATCH, DEC_SEQ, PLE_DIM), 1.0),
        'w_in': nrm(ks[9], (DEPTH, D_MODEL, IN_WIDTH), D_MODEL ** -0.5),
        'b_f': FOX_FORGET_BIAS + nrm(ks[10], (DEPTH, FOX_HEADS), 0.1),
        'w_a_up': nrm(ks[11], (DEPTH, GLA_GATE_RANK, GLA_HEADS * GLA_DK), GLA_GATE_RANK ** -0.5),
        'b_a': nrm(ks[12], (DEPTH, GLA_HEADS * GLA_DK), 0.1),
        'gla_norm': 1.0 + nrm(ks[13], (DEPTH, GLA_HEADS, GLA_DV), 0.05),
        'w_o_fox': nrm(ks[14], (DEPTH, fox_w, D_MODEL), fox_w ** -0.5),
        'w_o_gla': nrm(ks[15], (DEPTH, gla_v, D_MODEL), gla_v ** -0.5),
        'w_out': nrm(ks[16], (DEPTH, D_MODEL, D_MODEL), D_MODEL ** -0.5 * DEEPNORM_BETA),
        'ln1_g': 1.0 + nrm(ks[17], (DEPTH, D_MODEL), 0.05),
        'ln1_b': nrm(ks[18], (DEPTH, D_MODEL), 0.02),
        'w_router': nrm(ks[19], (DEPTH, D_MODEL, N_EXPERTS), D_MODEL ** -0.5),
        'b_router': nrm(ks[20], (DEPTH, N_EXPERTS), 0.01),
        'w_gu': nrm(ks[21], (DEPTH, N_EXPERTS, D_MODEL, 2 * D_FF), D_MODEL ** -0.5),
        'b_gu': nrm(ks[22], (DEPTH, N_EXPERTS, 2 * D_FF), 0.02),
        'w_down': nrm(ks[23], (DEPTH, N_EXPERTS, D_FF, D_MODEL), D_FF ** -0.5 * DEEPNORM_BETA),
        'b_down': nrm(ks[24], (DEPTH, N_EXPERTS, D_MODEL), 0.02),
        'w_ple_gate': nrm(ks[25], (DEPTH, D_MODEL, D_MODEL), D_MODEL ** -0.5),
        'b_ple_gate': nrm(ks[26], (DEPTH, D_MODEL), 0.02),
        'w_ple_proj': nrm(ks[27], (DEPTH, PLE_DIM, D_MODEL), PLE_DIM ** -0.5 * DEEPNORM_BETA),
        'ln2_g': 1.0 + nrm(ks[28], (DEPTH, D_MODEL), 0.05),
        'ln2_b': nrm(ks[29], (DEPTH, D_MODEL), 0.02),
    }


def reference(x_prompt, x_sample, cache_k, cache_v, cache_logf, state_gla, page_table, p_prompt, p_sample,
              w_in, b_f, w_a_up, b_a, gla_norm, w_o_fox, w_o_gla, w_out, ln1_g, ln1_b,
              w_router, b_router, w_gu, b_gu, w_down, b_down, w_ple_gate, b_ple_gate, w_ple_proj, ln2_g, ln2_b):
    y_prompt, y_sample = x_prompt, x_sample
    new_p, new_s = [], []
    for i in range(DEPTH):
        lw = tuple(w[i] for w in (w_in, b_f, w_a_up, b_a, gla_norm, w_o_fox, w_o_gla, w_out, ln1_g, ln1_b,
                                  w_router, b_router, w_gu, b_gu, w_down, b_down,
                                  w_ple_gate, b_ple_gate, w_ple_proj, ln2_g, ln2_b))
        s0 = jnp.zeros((x_prompt.shape[0], GLA_HEADS, GLA_DK, GLA_DV), jnp.float32)
        y_prompt, st_p = decoder_layer(y_prompt, p_prompt[i], fox_prompt, s0, *lw)
        attn_s = functools.partial(fox_sample, cache_k=cache_k, cache_v=cache_v, cache_logf=cache_logf,
                                   page_table=page_table, layer=i)
        y_sample, st_s = decoder_layer(y_sample, p_sample[i], attn_s, state_gla[i], *lw)
        new_p.append(st_p)
        new_s.append(st_s)
    k_prompt, v_prompt, logf_prompt, gla_prompt = [jnp.stack(t) for t in zip(*new_p)]
    k_sample, v_sample, logf_sample, gla_sample = [jnp.stack(t) for t in zip(*new_s)]
    return (y_prompt, y_sample, k_prompt, v_prompt, logf_prompt, gla_prompt,
            k_sample, v_sample, logf_sample, gla_sample)
```

```python
import functools

import numpy as np
import jax
import jax.numpy as jnp
from jax import lax
from jax.experimental import pallas as pl
from jax.experimental.pallas import tpu as pltpu

F32 = jnp.float32
BF16 = jnp.bfloat16

FOX_HEADS = 8
FOX_HEAD_DIM = 64
FOX_WIDTH = FOX_HEADS * FOX_HEAD_DIM
GLA_HEADS = 4
GLA_DK = 128
GLA_DV = 256
GLA_QK_WIDTH = GLA_HEADS * GLA_DK
GLA_V_WIDTH = GLA_HEADS * GLA_DV
GLA_GATE_RANK = 16
GLA_TAU = 16.0
GLA_CHUNK = 64
N_EXPERTS = 32
TOP_K = 4
SWIGLU_ALPHA = 1.702
SWIGLU_LIMIT = 7.0
DEPTH = 1
DEEPNORM_ALPHA = (2.0 * DEPTH) ** 0.25
LN_EPS = 1e-5
RMS_EPS = 1e-6

LANES = 128
BF16_SUBLANES = 16
PAGES_PER_STEP = 8
VMEM_LIMIT = 56 * 1024 * 1024

_NT = (((1,), (1,)), ((), ()))


def _dot(a, b):
    return jnp.dot(a, b, preferred_element_type=F32)


def _dot_nt(a, b):
    return lax.dot_general(a, b, _NT, preferred_element_type=F32)


def _split3(x):
    hi = x.astype(BF16)
    r = x - hi.astype(F32)
    mid = r.astype(BF16)
    lo = (r - mid.astype(F32)).astype(BF16)
    return hi, mid, lo


def _dot_f32_lhs(x, w01):
    m = x.shape[0]
    pad = -m % BF16_SUBLANES
    if pad:
        x = jnp.concatenate([x, jnp.zeros((pad, x.shape[1]), F32)], axis=0)
    hi, mid, lo = _split3(x)
    return (_dot(lo, w01) + _dot(mid, w01) + _dot(hi, w01))[:m]


def _dot_f32_rhs(w01, x):
    hi, mid, lo = _split3(x)
    return _dot(w01, lo) + _dot(w01, mid) + _dot(w01, hi)


def _sigmoid(x):
    return 1.0 / (1.0 + jnp.exp(-x))


def _log_sigmoid(x):
    return jnp.minimum(x, 0.0) - jnp.log1p(jnp.exp(-jnp.abs(x)))


def _iota(shape, dim):
    return lax.broadcasted_iota(jnp.int32, shape, dim)


def _params(sem):
    return pltpu.CompilerParams(dimension_semantics=sem, vmem_limit_bytes=VMEM_LIMIT)


_IN_BLOCK = 512
_J_FQ, _J_FK, _J_FV, _J_GQ, _J_GV, _J_GG, _J_GA, _J_GB, _J_END = 0, 1, 2, 3, 4, 6, 8, 10, 12


def _inproj_kernel(x_ref, w_ref, wf_ref, bf_ref, wlr_ref, wau_ref, ba_ref, wkt_ref, wlrt_ref, waut_ref, bat_ref,
                   fq_ref, fk_ref, fv_ref, fkb_ref, fvb_ref, gq_ref, gv_ref, gg_ref, ga_ref, gb_ref,
                   logf_ref, la_ref, gkt_ref, lat_ref, xb_ref):
    j = pl.program_id(1)

    @pl.when(j == 0)
    def _():
        xb = x_ref[...].astype(BF16)
        xb_ref[...] = xb
        ff = _dot(xb, wf_ref[...])[:, :FOX_HEADS] + bf_ref[...]
        logf_ref[...] = _log_sigmoid(ff)
        glr = _dot(xb, wlr_ref[...]).astype(BF16)
        la_ref[...] = _log_sigmoid(_dot(glr, wau_ref[...]) + ba_ref[...]) * (1.0 / GLA_TAU)
        gkt_ref[0] = _dot_nt(wkt_ref[...], xb)
        glrt = _dot_nt(wlrt_ref[...], xb).astype(BF16)
        lat_ref[0] = _log_sigmoid(_dot(waut_ref[...], glrt) + bat_ref[...]) * (1.0 / GLA_TAU)

    r = _dot(xb_ref[...], w_ref[...])

    @pl.when(j == _J_FQ)
    def _():
        fq_ref[...] = (r * FOX_HEAD_DIM ** -0.5).astype(BF16)

    @pl.when(j == _J_FK)
    def _():
        fk_ref[...] = r
        fkb_ref[...] = r.astype(BF16)

    @pl.when(j == _J_FV)
    def _():
        fv_ref[...] = r
        fvb_ref[...] = r.astype(BF16)

    @pl.when(j == _J_GQ)
    def _():
        gq_ref[...] = r * GLA_DK ** -0.5

    @pl.when((j >= _J_GV) & (j < _J_GG))
    def _():
        gv_ref[...] = r.astype(BF16)

    @pl.when((j >= _J_GG) & (j < _J_GA))
    def _():
        gg_ref[...] = r * _sigmoid(r)

    @pl.when((j >= _J_GA) & (j < _J_GB))
    def _():
        ga_ref[...] = _sigmoid(r)

    @pl.when(j >= _J_GB)
    def _():
        gb_ref[...] = _sigmoid(r)


def _in_projection(x2d, wp, n_seq):
    t, d = x2d.shape
    tm = min(_IN_BLOCK, t)
    seq = t // n_seq
    assert t % tm == 0 and seq % tm == 0 or n_seq == 1
    tiles_per_seq = max(seq // tm, 1)
    nb = _IN_BLOCK

    def col(start, n):
        return lambda i, j: (i, jnp.clip(j - start, 0, n - 1))

    def whole(shape):
        return pl.BlockSpec(shape, lambda i, j: (0,) * len(shape))

    def tmap(i, j):
        return (i // tiles_per_seq, 0, i % tiles_per_seq)

    out_shape = (
        jax.ShapeDtypeStruct((t, FOX_WIDTH), BF16),
        jax.ShapeDtypeStruct((t, FOX_WIDTH), F32),
        jax.ShapeDtypeStruct((t, FOX_WIDTH), F32),
        jax.ShapeDtypeStruct((t, FOX_WIDTH), BF16),
        jax.ShapeDtypeStruct((t, FOX_WIDTH), BF16),
        jax.ShapeDtypeStruct((t, GLA_QK_WIDTH), F32),
        jax.ShapeDtypeStruct((t, GLA_V_WIDTH), BF16),
        jax.ShapeDtypeStruct((t, GLA_V_WIDTH), F32),
        jax.ShapeDtypeStruct((t, d), F32),
        jax.ShapeDtypeStruct((t, d), F32),
        jax.ShapeDtypeStruct((t, FOX_HEADS), F32),
        jax.ShapeDtypeStruct((t, GLA_QK_WIDTH), F32),
        jax.ShapeDtypeStruct((n_seq, GLA_QK_WIDTH, seq), F32),
        jax.ShapeDtypeStruct((n_seq, GLA_QK_WIDTH, seq), F32),
    )
    out_specs = (
        pl.BlockSpec((tm, nb), col(_J_FQ, 1)),
        pl.BlockSpec((tm, nb), col(_J_FK, 1)),
        pl.BlockSpec((tm, nb), col(_J_FV, 1)),
        pl.BlockSpec((tm, nb), col(_J_FK, 1)),
        pl.BlockSpec((tm, nb), col(_J_FV, 1)),
        pl.BlockSpec((tm, nb), col(_J_GQ, 1)),
        pl.BlockSpec((tm, nb), col(_J_GV, 2)),
        pl.BlockSpec((tm, nb), col(_J_GG, 2)),
        pl.BlockSpec((tm, nb), col(_J_GA, 2)),
        pl.BlockSpec((tm, nb), col(_J_GB, 2)),
        pl.BlockSpec((tm, FOX_HEADS), lambda i, j: (i, 0)),
        pl.BlockSpec((tm, GLA_QK_WIDTH), lambda i, j: (i, 0)),
        pl.BlockSpec((1, GLA_QK_WIDTH, tm), tmap),
        pl.BlockSpec((1, GLA_QK_WIDTH, tm), tmap),
    )
    in_specs = [
        pl.BlockSpec((tm, d), lambda i, j: (i, 0)),
        pl.BlockSpec((d, nb), lambda i, j: (0, j)),
        whole((d, LANES)), whole((1, FOX_HEADS)), whole((d, LANES)), whole((LANES, GLA_QK_WIDTH)),
        whole((1, GLA_QK_WIDTH)), whole((GLA_QK_WIDTH, d)), whole((LANES, d)), whole((GLA_QK_WIDTH, LANES)),
        whole((GLA_QK_WIDTH, 1)),
    ]
    return pl.pallas_call(
        _inproj_kernel,
        out_shape=out_shape,
        grid=(t // tm, _J_END),
        in_specs=in_specs,
        out_specs=out_specs,
        scratch_shapes=[pltpu.VMEM((tm, d), BF16)],
        compiler_params=_params(("parallel", "arbitrary")),
        name="in_projection",
    )(x2d, wp["w_main"], wp["w_ff"], wp["b_f"], wp["w_lr"], wp["w_au"], wp["b_a"],
      wp["w_gk_t"], wp["w_lr_t"], wp["w_au_t"], wp["b_a_t"])


def _cumsum_kernel(l_ref, c_ref):
    n_chunks = l_ref.shape[1]
    upper = (_iota((LANES, LANES), 0) <= _iota((LANES, LANES), 1)).astype(BF16)

    def body(i, carry):
        cs = _dot_f32_lhs(l_ref[0, i], upper) + carry
        c_ref[0, i] = cs
        return cs[:, LANES - 1:LANES]

    lax.fori_loop(0, n_chunks, body, jnp.zeros((l_ref.shape[2], 1), F32))


def _prompt_cumsum(logf_chunks):
    n, nc, h, _ = logf_chunks.shape
    spec = pl.BlockSpec((1, nc, h, LANES), lambda b: (b, 0, 0, 0))
    return pl.pallas_call(
        _cumsum_kernel,
        out_shape=jax.ShapeDtypeStruct(logf_chunks.shape, F32),
        grid=(n,), in_specs=[spec], out_specs=spec,
        compiler_params=_params(("parallel",)),
        name="fox_cumsum",
    )(logf_chunks)


_FOX_BLOCK = 512
_HEADS_PER_GROUP = 4
_GROUP_WIDTH = _HEADS_PER_GROUP * FOX_HEAD_DIM


def _fox_prompt_kernel(qi_tab, ki_tab, q_ref, k_ref, v_ref, cq_ref, ck_ref, o_ref, m_ref, l_ref, acc_ref):
    step = pl.program_id(2)
    qi = qi_tab[step]
    ki = ki_tab[step]
    tq, tk = q_ref.shape[0], k_ref.shape[0]

    @pl.when(ki == 0)
    def _():
        m_ref[...] = jnp.full(m_ref.shape, -1e30, F32)
        l_ref[...] = jnp.zeros(l_ref.shape, F32)
        acc_ref[...] = jnp.zeros(acc_ref.shape, F32)

    q = q_ref[...]
    kb = k_ref[...]
    vb = v_ref[...]
    lane_head = _iota((1, _GROUP_WIDTH), 1) // FOX_HEAD_DIM
    causal = (_iota((tq, tk), 1) - _iota((tq, tk), 0)) <= (qi - ki) * tq

    corr_all = jnp.zeros((tq, _GROUP_WIDTH), F32)
    pv_all = jnp.zeros((tq, _GROUP_WIDTH), F32)
    for h in range(_HEADS_PER_GROUP):
        sel = lane_head == h
        qh = jnp.where(sel, q, jnp.zeros_like(q))
        s = _dot_nt(qh, kb) + (cq_ref[0, :, h:h + 1] - ck_ref[0, h:h + 1, :])
        s = jnp.where(causal, s, -jnp.inf)
        m_prev = m_ref[h]
        m_new = jnp.maximum(m_prev, jnp.max(s, axis=1, keepdims=True))
        p = jnp.exp(s - m_new)
        corr = jnp.exp(m_prev - m_new)
        l_ref[h] = corr * l_ref[h] + jnp.sum(p, axis=1, keepdims=True)
        m_ref[h] = m_new
        pv = _dot(p.astype(BF16), vb)
        corr_all = jnp.where(sel, corr, corr_all)
        pv_all = jnp.where(sel, pv, pv_all)
    acc_ref[...] = acc_ref[...] * corr_all + pv_all

    @pl.when(ki == qi)
    def _():
        l_all = jnp.ones((tq, _GROUP_WIDTH), F32)
        for h in range(_HEADS_PER_GROUP):
            l_all = jnp.where(lane_head == h, l_ref[h], l_all)
        o_ref[...] = (acc_ref[...] / l_all).astype(o_ref.dtype)


def _fox_prompt(fq, fkb, fvb, c_q, c_k, n, s_len):
    tq = min(_FOX_BLOCK, s_len)
    nq = s_len // tq
    pairs = [(a, b) for a in range(nq) for b in range(a + 1)]
    qi_tab = jnp.asarray(np.array([p[0] for p in pairs], np.int32))
    ki_tab = jnp.asarray(np.array([p[1] for p in pairs], np.int32))
    n_groups = FOX_HEADS // _HEADS_PER_GROUP
    grid_spec = pltpu.PrefetchScalarGridSpec(
        num_scalar_prefetch=2,
        grid=(n, n_groups, len(pairs)),
        in_specs=[
            pl.BlockSpec((tq, _GROUP_WIDTH), lambda b, g, t, qt, kt: (b * nq + qt[t], g)),
            pl.BlockSpec((tq, _GROUP_WIDTH), lambda b, g, t, qt, kt: (b * nq + kt[t], g)),
            pl.BlockSpec((tq, _GROUP_WIDTH), lambda b, g, t, qt, kt: (b * nq + kt[t], g)),
            pl.BlockSpec((1, tq, _HEADS_PER_GROUP), lambda b, g, t, qt, kt: (b * n_groups + g, qt[t], 0)),
            pl.BlockSpec((1, _HEADS_PER_GROUP, tq), lambda b, g, t, qt, kt: (b * n_groups + g, 0, kt[t])),
        ],
        out_specs=pl.BlockSpec((tq, _GROUP_WIDTH), lambda b, g, t, qt, kt: (b * nq + qt[t], g)),
        scratch_shapes=[
            pltpu.VMEM((_HEADS_PER_GROUP, tq, 1), F32),
            pltpu.VMEM((_HEADS_PER_GROUP, tq, 1), F32),
            pltpu.VMEM((tq, _GROUP_WIDTH), F32),
        ],
    )
    return pl.pallas_call(
        _fox_prompt_kernel,
        out_shape=jax.ShapeDtypeStruct((n * s_len, FOX_WIDTH), BF16),
        grid_spec=grid_spec,
        compiler_params=_params(("parallel", "parallel", "arbitrary")),
        name="fox_prompt",
    )(qi_tab, ki_tab, fq, fkb, fvb, c_q, c_k)


def _rows_per_head(x, reps):
    return jnp.concatenate([jnp.broadcast_to(x[h:h + 1], (reps, x.shape[1])) for h in range(x.shape[0])], axis=0)


def _fox_sample_kernel(pt_ref, q_ref, kn_ref, vn_ref, lfn_ref, *refs, n_new, page):
    del pt_ref
    pp = PAGES_PER_STEP
    k_refs, v_refs, lf_refs = refs[:pp], refs[pp:2 * pp], refs[2 * pp:3 * pp]
    o_ref, qbd_ref, cn_ref, carry_ref, m_ref, l_ref, acc_ref, kbf_ref, vbf_ref = refs[3 * pp:]
    step = pl.program_id(1)
    rows = FOX_HEADS * n_new
    later = (_iota((page, page), 0) > _iota((page, page), 1)).astype(BF16)

    def attend(s, vals):
        m_prev = m_ref[...]
        m_new = jnp.maximum(m_prev, jnp.max(s, axis=1, keepdims=True))
        p = jnp.exp(s - m_new)
        corr = jnp.exp(m_prev - m_new)
        l_ref[...] = corr * l_ref[...] + jnp.sum(p, axis=1, keepdims=True)
        m_ref[...] = m_new
        acc_ref[...] = acc_ref[...] * corr + _dot(p.astype(BF16), vals)

    @pl.when(step == 0)
    def _():
        q = q_ref[...]
        q_rows = jnp.concatenate([q] * FOX_HEADS, axis=0)
        own = (_iota((rows, FOX_WIDTH), 0) // n_new) == (_iota((rows, FOX_WIDTH), 1) // FOX_HEAD_DIM)
        qbd = jnp.where(own, q_rows, jnp.zeros_like(q_rows))
        qbd_ref[...] = qbd
        incl = (_iota((LANES, LANES), 0) <= _iota((LANES, LANES), 1)).astype(BF16)
        cn = _dot_f32_lhs(lfn_ref[0], incl)
        cn_rows = _rows_per_head(cn, n_new)
        mine = (_iota((rows, LANES), 0) % n_new) == _iota((rows, LANES), 1)
        cn_col = jnp.sum(jnp.where(mine, cn_rows, 0.0), axis=1, keepdims=True)
        cn_ref[...] = cn_col
        carry_ref[...] = jnp.zeros(carry_ref.shape, F32)
        m_ref[...] = jnp.full(m_ref.shape, -1e30, F32)
        l_ref[...] = jnp.zeros(l_ref.shape, F32)
        acc_ref[...] = jnp.zeros(acc_ref.shape, F32)
        zero_rows = jnp.zeros((page - n_new, FOX_WIDTH), F32)
        k_new = jnp.concatenate([kn_ref[...].astype(F32), zero_rows], axis=0).astype(BF16)
        v_new = jnp.concatenate([vn_ref[...].astype(F32), zero_rows], axis=0).astype(BF16)
        s = _dot_nt(qbd, k_new) + (cn_col - cn_rows)
        visible = _iota((rows, LANES), 1) <= (_iota((rows, LANES), 0) % n_new)
        attend(jnp.where(visible, s, -jnp.inf), v_new)

    bias = []
    carry = carry_ref[...]
    for i in range(pp):
        kbf_ref[i * page:(i + 1) * page] = k_refs[i][0].astype(BF16)
        vbf_ref[i * page:(i + 1) * page] = v_refs[i][0].astype(BF16)
        lf = lf_refs[i][0]
        bias.append(_rows_per_head(_dot_f32_lhs(lf, later) + carry, n_new))
        carry = carry + jnp.sum(lf, axis=1, keepdims=True)
    carry_ref[...] = carry
    s = _dot_nt(qbd_ref[...], kbf_ref[...]) + (jnp.concatenate(bias, axis=1) + cn_ref[...])
    attend(s, vbf_ref[...])

    @pl.when(step == pl.num_programs(1) - 1)
    def _():
        o = acc_ref[...] / l_ref[...]
        lane_head = _iota((n_new, FOX_WIDTH), 1) // FOX_HEAD_DIM
        out = jnp.zeros((n_new, FOX_WIDTH), F32)
        for h in range(FOX_HEADS):
            out = jnp.where(lane_head == h, o[h * n_new:(h + 1) * n_new], out)
        o_ref[...] = out.astype(o_ref.dtype)


def _fox_sample(fq, fkb, fvb, logf_new_t, cache_k, cache_v, cache_logf_t, page_table, n_new):
    b, n_pages = page_table.shape
    page = cache_k.shape[1]
    pp = PAGES_PER_STEP
    assert n_pages % pp == 0 and page == LANES
    rows = FOX_HEADS * n_new

    def page_spec(shape, i):
        return pl.BlockSpec(shape, lambda s, t, pt: (pt[s, n_pages - 1 - (t * pp + i)], 0, 0))

    tok = lambda s, t, pt: (s, 0)
    grid_spec = pltpu.PrefetchScalarGridSpec(
        num_scalar_prefetch=1,
        grid=(b, n_pages // pp),
        in_specs=([pl.BlockSpec((n_new, FOX_WIDTH), tok)] * 3
                  + [pl.BlockSpec((1, FOX_HEADS, LANES), lambda s, t, pt: (s, 0, 0))]
                  + [page_spec((1, page, FOX_WIDTH), i) for i in range(pp)]
                  + [page_spec((1, page, FOX_WIDTH), i) for i in range(pp)]
                  + [page_spec((1, FOX_HEADS, page), i) for i in range(pp)]),
        out_specs=pl.BlockSpec((n_new, FOX_WIDTH), tok),
        scratch_shapes=[
            pltpu.VMEM((rows, FOX_WIDTH), BF16),
            pltpu.VMEM((rows, 1), F32),
            pltpu.VMEM((FOX_HEADS, 1), F32),
            pltpu.VMEM((rows, 1), F32),
            pltpu.VMEM((rows, 1), F32),
            pltpu.VMEM((rows, FOX_WIDTH), F32),
            pltpu.VMEM((pp * page, FOX_WIDTH), BF16),
            pltpu.VMEM((pp * page, FOX_WIDTH), BF16),
        ],
    )
    return pl.pallas_call(
        functools.partial(_fox_sample_kernel, n_new=n_new, page=page),
        out_shape=jax.ShapeDtypeStruct((b * n_new, FOX_WIDTH), BF16),
        grid_spec=grid_spec,
        compiler_params=_params(("parallel", "arbitrary")),
        name="fox_sample",
    )(page_table, fq, fkb, fvb, logf_new_t, *([cache_k] * pp), *([cache_v] * pp), *([cache_logf_t] * pp))


_GLA_CHUNKS_PER_STEP = 4


def _gla_kernel(q_ref, la_ref, v_ref, gg_ref, kt_ref, lat_ref, gn_ref, s0_ref, og_ref, sfin_ref, st_ref, *, cps):
    c = GLA_CHUNK
    step = pl.program_id(1)

    @pl.when(step == 0)
    def _():
        st_ref[...] = s0_ref[0]

    lower = (_iota((c, c), 0) >= _iota((c, c), 1))
    lower_b = lower.astype(BF16)
    upper_b = (_iota((c, c), 0) <= _iota((c, c), 1)).astype(BF16)
    for ci in range(cps):
        rs = slice(ci * c, (ci + 1) * c)
        for h in range(GLA_HEADS):
            ks = slice(h * GLA_DK, (h + 1) * GLA_DK)
            vs = slice(h * GLA_DV, (h + 1) * GLA_DV)
            b = _dot_f32_rhs(lower_b, la_ref[rs, ks])
            bt = _dot_f32_lhs(lat_ref[0, ks, rs], upper_b)
            b_last = bt[:, c - 1:c]
            kt = kt_ref[0, ks, rs]
            q_dec = (q_ref[rs, ks] * jnp.exp(b)).astype(BF16)
            k_dec_t = (kt * jnp.exp(-bt)).astype(BF16)
            k_end_t = (kt * jnp.exp(b_last - bt)).astype(BF16)
            v = v_ref[rs, vs]
            attn = jnp.where(lower, _dot(q_dec, k_dec_t), 0.0)
            state = st_ref[h]
            o = _dot(attn.astype(BF16), v) + _dot(q_dec, state.astype(BF16))
            st_ref[h] = state * jnp.exp(b_last) + _dot(k_end_t, v)
            scale = lax.rsqrt(jnp.mean(jnp.square(o), axis=1, keepdims=True) + RMS_EPS)
            og_ref[rs, vs] = (o * scale * gn_ref[:, vs] * gg_ref[rs, vs]).astype(og_ref.dtype)

    @pl.when(step == pl.num_programs(1) - 1)
    def _():
        sfin_ref[0] = st_ref[...]


def _gla(gq, la, gv, gg, gk_t, la_t, gla_norm, s0, n_seq, seq):
    cps = min(_GLA_CHUNKS_PER_STEP, seq // GLA_CHUNK)
    ts = cps * GLA_CHUNK
    steps = seq // ts
    tok = lambda s, t: (s * steps + t, 0)
    return pl.pallas_call(
        functools.partial(_gla_kernel, cps=cps),
        out_shape=(jax.ShapeDtypeStruct((n_seq * seq, GLA_V_WIDTH), BF16),
                   jax.ShapeDtypeStruct((n_seq, GLA_HEADS, GLA_DK, GLA_DV), F32)),
        grid=(n_seq, steps),
        in_specs=[
            pl.BlockSpec((ts, GLA_QK_WIDTH), tok),
            pl.BlockSpec((ts, GLA_QK_WIDTH), tok),
            pl.BlockSpec((ts, GLA_V_WIDTH), tok),
            pl.BlockSpec((ts, GLA_V_WIDTH), tok),
            pl.BlockSpec((1, GLA_QK_WIDTH, ts), lambda s, t: (s, 0, t)),
            pl.BlockSpec((1, GLA_QK_WIDTH, ts), lambda s, t: (s, 0, t)),
            pl.BlockSpec((1, GLA_V_WIDTH), lambda s, t: (0, 0)),
            pl.BlockSpec((1, GLA_HEADS, GLA_DK, GLA_DV), lambda s, t: (s, 0, 0, 0)),
        ],
        out_specs=(pl.BlockSpec((ts, GLA_V_WIDTH), tok),
                   pl.BlockSpec((1, GLA_HEADS, GLA_DK, GLA_DV), lambda s, t: (s, 0, 0, 0))),
        scratch_shapes=[pltpu.VMEM((GLA_HEADS, GLA_DK, GLA_DV), F32)],
        compiler_params=_params(("parallel", "arbitrary")),
        name="gla",
    )(gq, la, gv, gg, gk_t, la_t, gla_norm, s0)


_MIX_BLOCK = 512


def _pack_bf16_pairs(x):
    w = x.shape[1] // 2
    bits = lax.bitcast_convert_type(x.astype(BF16).astype(F32), jnp.uint32)
    return (bits[:, :w] >> 16) | (bits[:, w:] & jnp.uint32(0xFFFF0000))


def _unpack_bf16_pairs(u):
    lo = lax.bitcast_convert_type(u << 16, F32).astype(BF16)
    hi = lax.bitcast_convert_type(u & jnp.uint32(0xFFFF0000), F32).astype(BF16)
    return lo, hi


def _layer_norm(z, g, b):
    mu = jnp.mean(z, axis=1, keepdims=True)
    zc = z - mu
    var = jnp.mean(jnp.square(zc), axis=1, keepdims=True)
    return zc * lax.rsqrt(var + LN_EPS) * g + b


def _mix_kernel(of_ref, og_ref, ga_ref, gb_ref, x_ref, p_ref, wof_ref, wog_ref, wout_ref, wpg_ref, bpg_ref,
                wpp_ref, g1_ref, b1_ref, wr_ref, br_ref, base_ref, x1p_ref, lg_ref):
    branch_a = _dot(of_ref[...], wof_ref[...])
    branch_b = _dot(og_ref[...], wog_ref[...])
    merged = ga_ref[...] * branch_a + gb_ref[...] * branch_b
    mix = _dot(merged.astype(BF16), wout_ref[...])
    x1 = _layer_norm(DEEPNORM_ALPHA * x_ref[...] + mix, g1_ref[...], b1_ref[...])
    gate = _sigmoid(_dot(x1.astype(BF16), wpg_ref[...]) + bpg_ref[...])
    ple = gate * _dot(p_ref[...].astype(BF16), wpp_ref[...])
    base_ref[...] = DEEPNORM_ALPHA * x1 + ple
    x1p_ref[...] = _pack_bf16_pairs(x1)
    logits = jnp.dot(x1, wr_ref[...], precision=lax.Precision.HIGHEST, preferred_element_type=F32)
    lg_ref[...] = logits.T[:N_EXPERTS] + br_ref[...]


def _mix(o_fox, og, gate_a, gate_b, x2d, p2d, wp):
    t, d = x2d.shape
    tm = min(_MIX_BLOCK, t)
    ple_dim = p2d.shape[1]
    row = lambda w: pl.BlockSpec((tm, w), lambda i: (i, 0))
    whole = lambda shape: pl.BlockSpec(shape, lambda i: (0,) * len(shape))
    return pl.pallas_call(
        _mix_kernel,
        out_shape=(jax.ShapeDtypeStruct((t, d), F32),
                   jax.ShapeDtypeStruct((t, d // 2), jnp.uint32),
                   jax.ShapeDtypeStruct((N_EXPERTS, t), F32)),
        grid=(t // tm,),
        in_specs=[row(FOX_WIDTH), row(GLA_V_WIDTH), row(d), row(d), row(d), row(ple_dim),
                  whole((FOX_WIDTH, d)), whole((GLA_V_WIDTH, d)), whole((d, d)), whole((d, d)), whole((1, d)),
                  whole((ple_dim, d)), whole((1, d)), whole((1, d)), whole((d, LANES)), whole((N_EXPERTS, 1))],
        out_specs=(row(d), row(d // 2), pl.BlockSpec((N_EXPERTS, tm), lambda i: (0, i))),
        compiler_params=_params(("parallel",)),
        name="mix_ln_router",
    )(o_fox, og, gate_a, gate_b, x2d, p2d, wp["w_o_fox"], wp["w_o_gla"], wp["w_out"], wp["w_ple_gate"],
      wp["b_ple_gate"], wp["w_ple_proj"], wp["ln1_g"], wp["ln1_b"], wp["w_router"], wp["b_router"])


_ROUTE_BLOCK = 512


def _route_kernel(lg_ref, e_ref, w_ref, rank_ref, cnt_ref, carry_ref):
    i = pl.program_id(0)
    tn = lg_ref.shape[1]

    @pl.when(i == 0)
    def _():
        carry_ref[...] = jnp.zeros(carry_ref.shape, F32)

    lg = lg_ref[...]
    eidx = _iota((N_EXPERTS, tn), 0).astype(F32)
    vals, picks = [], []
    chosen_f = jnp.zeros((N_EXPERTS, tn), F32)
    for _ in range(TOP_K):
        mx = jnp.max(lg, axis=0, keepdims=True)
        idx = jnp.min(jnp.where(lg == mx, eidx, float(N_EXPERTS)), axis=0, keepdims=True)
        pick = eidx == idx
        vals.append(mx)
        picks.append(pick)
        chosen_f = jnp.where(pick, 1.0, chosen_f)
        lg = jnp.where(pick, -jnp.inf, lg)
    ex = [jnp.exp(v - vals[0]) for v in vals]
    den = ex[0] + ex[1] + ex[2] + ex[3]
    upper = (_iota((tn, tn), 0) <= _iota((tn, tn), 1)).astype(BF16)
    before = carry_ref[...] + _dot(chosen_f.astype(BF16), upper) - chosen_f
    for k in range(TOP_K):
        e_ref[k:k + 1, :] = jnp.sum(jnp.where(picks[k], eidx, 0.0), axis=0, keepdims=True).astype(jnp.int32)
        w_ref[k:k + 1, :] = ex[k] / den
        rank_ref[k:k + 1, :] = jnp.sum(jnp.where(picks[k], before, 0.0), axis=0, keepdims=True).astype(jnp.int32)
    carry_ref[...] = carry_ref[...] + jnp.sum(chosen_f, axis=1, keepdims=True)
    cnt_ref[...] = jnp.broadcast_to(carry_ref[...], cnt_ref.shape)


def _route(logits_t):
    t = logits_t.shape[1]
    tn = min(_ROUTE_BLOCK, t)
    blk = pl.BlockSpec((TOP_K, tn), lambda i: (0, i))
    return pl.pallas_call(
        _route_kernel,
        out_shape=(jax.ShapeDtypeStruct((TOP_K, t), jnp.int32),
                   jax.ShapeDtypeStruct((TOP_K, t), F32),
                   jax.ShapeDtypeStruct((TOP_K, t), jnp.int32),
                   jax.ShapeDtypeStruct((N_EXPERTS, LANES), F32)),
        grid=(t // tn,),
        in_specs=[pl.BlockSpec((N_EXPERTS, tn), lambda i: (0, i))],
        out_specs=(blk, blk, blk, pl.BlockSpec((N_EXPERTS, LANES), lambda i: (0, 0))),
        scratch_shapes=[pltpu.VMEM((N_EXPERTS, 1), F32)],
        compiler_params=_params(("arbitrary",)),
        name="moe_route",
    )(logits_t)


def _slots_kernel(e_ref, rank_ref, cnt_ref, dest_ref, be_ref, *, block_rows):
    tn = e_ref.shape[1]
    cnt = cnt_ref[...]
    padded = jnp.ceil(cnt * (1.0 / block_rows)) * block_rows
    incl = (_iota((N_EXPERTS, N_EXPERTS), 0) >= _iota((N_EXPERTS, N_EXPERTS), 1)).astype(BF16)
    ends = _dot_f32_rhs(incl, padded)
    starts = (ends - padded)[:, 0:1]
    eidx = _iota((N_EXPERTS, tn), 0)
    for k in range(TOP_K):
        first = jnp.sum(jnp.where(eidx == e_ref[k:k + 1, :], starts, 0.0), axis=0, keepdims=True)
        dest_ref[k:k + 1, :] = first.astype(jnp.int32) + rank_ref[k:k + 1, :]
    nb = be_ref.shape[1]
    row_start = (_iota((N_EXPERTS, nb), 1) * block_rows).astype(F32)
    owner = jnp.sum(jnp.where(ends[:, 0:1] <= row_start, 1.0, 0.0), axis=0, keepdims=True)
    be_ref[...] = jnp.minimum(owner, N_EXPERTS - 1.0).astype(jnp.int32)


def _slots(experts, ranks, counts, block_rows, n_blocks):
    t = experts.shape[1]
    tn = min(_ROUTE_BLOCK, t)
    nb_pad = -(-n_blocks // LANES) * LANES
    blk = pl.BlockSpec((TOP_K, tn), lambda i: (0, i))
    return pl.pallas_call(
        functools.partial(_slots_kernel, block_rows=block_rows),
        out_shape=(jax.ShapeDtypeStruct((TOP_K, t), jnp.int32),
                   jax.ShapeDtypeStruct((1, nb_pad), jnp.int32)),
        grid=(t // tn,),
        in_specs=[blk, blk, pl.BlockSpec((N_EXPERTS, LANES), lambda i: (0, 0))],
        out_specs=(blk, pl.BlockSpec((1, nb_pad), lambda i: (0, 0))),
        compiler_params=_params(("arbitrary",)),
        name="moe_slots",
    )(experts, ranks, counts)


_DISPATCH_BLOCK = 512


def _dispatch_kernel(dest_ref, x_ref, zero_ref, xb_ref, sem):
    del zero_ref
    i = pl.program_id(0)
    tn = dest_ref.shape[1]
    base = i * tn

    def issue(j, carry):
        for k in range(TOP_K):
            pltpu.make_async_copy(x_ref.at[pl.ds(base + j, 1)], xb_ref.at[pl.ds(dest_ref[k, j], 1)], sem).start()
        return carry

    lax.fori_loop(0, tn, issue, 0)
    for k in range(TOP_K):
        pltpu.make_async_copy(x_ref.at[pl.ds(0, tn)], xb_ref.at[pl.ds(0, tn)], sem).wait()


def _dispatch(x_packed, dest, n_slots):
    t, w = x_packed.shape
    tn = min(_DISPATCH_BLOCK, t)
    zeros = jnp.zeros((n_slots, w), x_packed.dtype)
    return pl.pallas_call(
        _dispatch_kernel,
        out_shape=jax.ShapeDtypeStruct((n_slots, w), x_packed.dtype),
        grid=(t // tn,),
        in_specs=[pl.BlockSpec((TOP_K, tn), lambda i: (0, i), memory_space=pltpu.SMEM),
                  pl.BlockSpec(memory_space=pl.ANY),
                  pl.BlockSpec(memory_space=pl.ANY)],
        out_specs=pl.BlockSpec(memory_space=pl.ANY),
        scratch_shapes=[pltpu.SemaphoreType.DMA(())],
        input_output_aliases={2: 0},
        compiler_params=_params(("arbitrary",)),
        name="moe_dispatch",
    )(dest, x_packed, zeros)


def _expert_kernel(be_ref, x_ref, wgu_ref, bgu_ref, wd_ref, bd_ref, y_ref):
    del be_ref
    lo, hi = _unpack_bf16_pairs(x_ref[...])
    half = lo.shape[1]
    h = _dot(lo, wgu_ref[0, :half]) + _dot(hi, wgu_ref[0, half:]) + bgu_ref[0]
    d_ff = h.shape[1] // 2
    gate = jnp.minimum(h[:, :d_ff], SWIGLU_LIMIT)
    up = jnp.clip(h[:, d_ff:], -SWIGLU_LIMIT, SWIGLU_LIMIT)
    hid = (up + 1.0) * gate * _sigmoid(SWIGLU_ALPHA * gate)
    y_ref[...] = _dot(hid.astype(BF16), wd_ref[0]) + bd_ref[0]


def _experts(xb, block_expert, w_gu, b_gu, w_down, b_down, block_rows, n_blocks):
    d = w_down.shape[2]
    wexp = lambda *shape: pl.BlockSpec((1,) + shape, lambda i, be: (be[i],) + (0,) * len(shape))
    grid_spec = pltpu.PrefetchScalarGridSpec(
        num_scalar_prefetch=1,
        grid=(n_blocks,),
        in_specs=[pl.BlockSpec((block_rows, xb.shape[1]), lambda i, be: (i, 0)),
                  wexp(*w_gu.shape[1:]), wexp(*b_gu.shape[1:]), wexp(*w_down.shape[1:]), wexp(*b_down.shape[1:])],
        out_specs=pl.BlockSpec((block_rows, d), lambda i, be: (i, 0)),
    )
    return pl.pallas_call(
        _expert_kernel,
        out_shape=jax.ShapeDtypeStruct((n_blocks * block_rows, d), F32),
        grid_spec=grid_spec,
        compiler_params=_params(("parallel",)),
        name="moe_experts",
    )(block_expert, xb, w_gu, b_gu, w_down, b_down)


_COMBINE_BLOCK = 256


def _combine_kernel(dest_ref, yb_ref, w_ref, base_ref, g_ref, b_ref, o_ref, buf_ref, sem):
    tm = base_ref.shape[0]

    def issue(j, carry):
        for k in range(TOP_K):
            pltpu.make_async_copy(yb_ref.at[pl.ds(dest_ref[k, j], 1)], buf_ref.at[k, pl.ds(j, 1)], sem).start()
        return carry

    lax.fori_loop(0, tm, issue, 0)
    for k in range(TOP_K):
        pltpu.make_async_copy(yb_ref.at[pl.ds(0, tm)], buf_ref.at[k], sem).wait()
    y = base_ref[...]
    for k in range(TOP_K):
        y = y + buf_ref[k] * w_ref[:, k:k + 1]
    o_ref[...] = _layer_norm(y, g_ref[...], b_ref[...])


def _combine(yb, dest, w_tok, base, ln_g, ln_b):
    t, d = base.shape
    tm = min(_COMBINE_BLOCK, t)
    return pl.pallas_call(
        _combine_kernel,
        out_shape=jax.ShapeDtypeStruct((t, d), F32),
        grid=(t // tm,),
        in_specs=[pl.BlockSpec((TOP_K, tm), lambda i: (0, i), memory_space=pltpu.SMEM),
                  pl.BlockSpec(memory_space=pl.ANY),
                  pl.BlockSpec((tm, TOP_K), lambda i: (i, 0)),
                  pl.BlockSpec((tm, d), lambda i: (i, 0)),
                  pl.BlockSpec((1, d), lambda i: (0, 0)),
                  pl.BlockSpec((1, d), lambda i: (0, 0))],
        out_specs=pl.BlockSpec((tm, d), lambda i: (i, 0)),
        scratch_shapes=[pltpu.VMEM((TOP_K, tm, d), F32), pltpu.SemaphoreType.DMA(())],
        compiler_params=_params(("arbitrary",)),
        name="moe_combine",
    )(dest, yb, w_tok, base, ln_g, ln_b)


def _prepare_weights(w_in, b_f, w_a_up, b_a, gla_norm, w_o_fox, w_o_gla, w_out, ln1_g, ln1_b, w_router, b_router,
                     w_gu, b_gu, w_down, b_down, w_ple_gate, b_ple_gate, w_ple_proj, ln2_g, ln2_b):
    d = w_in.shape[0]
    widths = (FOX_WIDTH, FOX_WIDTH, FOX_WIDTH, FOX_HEADS, GLA_QK_WIDTH, GLA_QK_WIDTH, GLA_V_WIDTH, GLA_V_WIDTH,
              GLA_GATE_RANK, d, d)
    offs = np.cumsum((0,) + widths)
    fq, fk, fv, ff, gq, gk, gv, gg, glr, ga, gb = [w_in[:, offs[i]:offs[i + 1]] for i in range(len(widths))]

    def pad_cols(w, n):
        return jnp.pad(w, ((0, 0), (0, n - w.shape[1])))

    row = lambda v: v.reshape(1, -1).astype(F32)
    w_au_pad = jnp.pad(w_a_up, ((0, LANES - GLA_GATE_RANK), (0, 0)))
    return {
        "w_main": jnp.concatenate([fq, fk, fv, gq, gv, gg, ga, gb], axis=1).astype(BF16),
        "w_ff": pad_cols(ff, LANES).astype(BF16),
        "b_f": row(b_f),
        "w_lr": pad_cols(glr, LANES).astype(BF16),
        "w_au": w_au_pad.astype(BF16),
        "b_a": row(b_a),
        "w_gk_t": gk.T.astype(BF16),
        "w_lr_t": pad_cols(glr, LANES).T.astype(BF16),
        "w_au_t": w_au_pad.T.astype(BF16),
        "b_a_t": b_a.reshape(-1, 1).astype(F32),
        "gla_norm": row(gla_norm),
        "w_o_fox": w_o_fox.astype(BF16), "w_o_gla": w_o_gla.astype(BF16), "w_out": w_out.astype(BF16),
        "ln1_g": row(ln1_g), "ln1_b": row(ln1_b),
        "w_router": pad_cols(w_router, LANES).astype(F32), "b_router": b_router.reshape(-1, 1).astype(F32),
        "w_gu": w_gu.astype(BF16), "b_gu": b_gu.reshape(N_EXPERTS, 1, -1).astype(F32),
        "w_down": w_down.astype(BF16), "b_down": b_down.reshape(N_EXPERTS, 1, -1).astype(F32),
        "w_ple_gate": w_ple_gate.astype(BF16), "b_ple_gate": row(b_ple_gate),
        "w_ple_proj": w_ple_proj.astype(BF16),
        "ln2_g": row(ln2_g), "ln2_b": row(ln2_b),
    }


def _moe_and_norm(base, x1_packed, logits_t, wp, block_rows):
    t = base.shape[0]
    n_blocks = -(-(t * TOP_K) // block_rows) + N_EXPERTS
    experts, weights, ranks, counts = _route(logits_t)
    dest, block_expert = _slots(experts, ranks, counts, block_rows, n_blocks)
    xb = _dispatch(x1_packed, dest, n_blocks * block_rows)
    yb = _experts(xb, block_expert[0, :n_blocks], wp["w_gu"], wp["b_gu"], wp["w_down"], wp["b_down"],
                  block_rows, n_blocks)
    return _combine(yb, dest, weights.T, base, wp["ln2_g"], wp["ln2_b"])


def _prompt_layer(x, p, wp):
    n, s_len, d = x.shape
    x2d = x.reshape(n * s_len, d)
    fq, fk, fv, fkb, fvb, gq, gv, gg, ga, gb, logf, la, gk_t, la_t = _in_projection(x2d, wp, n)
    lf = logf.reshape(n, s_len // LANES, LANES, FOX_HEADS).transpose(0, 1, 3, 2)
    c = _prompt_cumsum(lf).transpose(0, 2, 1, 3).reshape(n, FOX_HEADS, s_len)
    n_groups = FOX_HEADS // _HEADS_PER_GROUP
    c_k = c.reshape(n * n_groups, _HEADS_PER_GROUP, s_len)
    c_q = c_k.transpose(0, 2, 1)
    o_fox = _fox_prompt(fq, fkb, fvb, c_q, c_k, n, s_len)
    s0 = jnp.zeros((n, GLA_HEADS, GLA_DK, GLA_DV), F32)
    og, s_fin = _gla(gq, la, gv, gg, gk_t, la_t, wp["gla_norm"], s0, n, s_len)
    base, x1p, logits_t = _mix(o_fox, og, ga, gb, x2d, p.reshape(n * s_len, -1), wp)
    y = _moe_and_norm(base, x1p, logits_t, wp, block_rows=256)
    return (y.reshape(n, s_len, d), fk.reshape(1, n, s_len, FOX_HEADS, FOX_HEAD_DIM),
            fv.reshape(1, n, s_len, FOX_HEADS, FOX_HEAD_DIM), logf.reshape(1, n, s_len, FOX_HEADS), s_fin[None])


def _sample_layer(x, p, cache_k, cache_v, cache_logf, state, page_table, wp):
    b, n_new, d = x.shape
    t = b * n_new
    x2d = x.reshape(t, d)
    fq, fk, fv, fkb, fvb, gq, gv, gg, ga, gb, logf, la, gk_t, la_t = _in_projection(x2d, wp, 1)
    pool, page = cache_k.shape[0], cache_k.shape[1]
    logf_new_t = jnp.pad(logf.reshape(b, n_new, FOX_HEADS).transpose(0, 2, 1), ((0, 0), (0, 0), (0, LANES - n_new)))
    o_fox = _fox_sample(fq, fkb, fvb, logf_new_t, cache_k.reshape(pool, page, FOX_WIDTH),
                        cache_v.reshape(pool, page, FOX_WIDTH), cache_logf.transpose(0, 2, 1), page_table, n_new)
    c = GLA_CHUNK
    pad_tok = lambda a: jnp.pad(a.reshape(b, n_new, -1), ((0, 0), (0, c - n_new), (0, 0))).reshape(b * c, -1)
    pad_time = lambda a: jnp.pad(a.reshape(-1, b, n_new).transpose(1, 0, 2), ((0, 0), (0, 0), (0, c - n_new)))
    og, s_fin = _gla(pad_tok(gq), pad_tok(la), pad_tok(gv), pad_tok(gg), pad_time(gk_t), pad_time(la_t),
                     wp["gla_norm"], state, b, c)
    og = og.reshape(b, c, -1)[:, :n_new].reshape(t, -1)
    base, x1p, logits_t = _mix(o_fox, og, ga, gb, x2d, p.reshape(t, -1), wp)
    y = _moe_and_norm(base, x1p, logits_t, wp, block_rows=128)
    return (y.reshape(b, n_new, d), fk.reshape(1, b, n_new, FOX_HEADS, FOX_HEAD_DIM),
            fv.reshape(1, b, n_new, FOX_HEADS, FOX_HEAD_DIM), logf.reshape(1, b, n_new, FOX_HEADS), s_fin[None])


def kernel(x_prompt, x_sample, cache_k, cache_v, cache_logf, state_gla, page_table, p_prompt, p_sample, w_in, b_f,
           w_a_up, b_a, gla_norm, w_o_fox, w_o_gla, w_out, ln1_g, ln1_b, w_router, b_router, w_gu, b_gu, w_down,
           b_down, w_ple_gate, b_ple_gate, w_ple_proj, ln2_g, ln2_b):
    assert w_in.shape[0] == DEPTH
    wp = _prepare_weights(w_in[0], b_f[0], w_a_up[0], b_a[0], gla_norm[0], w_o_fox[0], w_o_gla[0], w_out[0],
                          ln1_g[0], ln1_b[0], w_router[0], b_router[0], w_gu[0], b_gu[0], w_down[0], b_down[0],
                          w_ple_gate[0], b_ple_gate[0], w_ple_proj[0], ln2_g[0], ln2_b[0])
    y_p, k_p, v_p, lf_p, s_p = _prompt_layer(x_prompt, p_prompt[0], wp)
    y_s, k_s, v_s, lf_s, s_s = _sample_layer(x_sample, p_sample[0], cache_k[0], cache_v[0], cache_logf[0],
                                             state_gla[0], page_table, wp)
    return (y_p, y_s, k_p, v_p, lf_p, s_p, k_s, v_s, lf_s, s_s)
```

```python
import functools

import numpy as np
import jax
import jax.numpy as jnp
from jax import lax
from jax.experimental import pallas as pl
from jax.experimental.pallas import tpu as pltpu

F32 = jnp.float32
BF16 = jnp.bfloat16

FOX_HEADS = 8
FOX_HEAD_DIM = 64
FOX_WIDTH = FOX_HEADS * FOX_HEAD_DIM
GLA_HEADS = 4
GLA_DK = 128
GLA_DV = 256
GLA_QK_WIDTH = GLA_HEADS * GLA_DK
GLA_V_WIDTH = GLA_HEADS * GLA_DV
GLA_GATE_RANK = 16
GLA_TAU = 16.0
GLA_CHUNK = 64
N_EXPERTS = 32
TOP_K = 4
SWIGLU_ALPHA = 1.702
SWIGLU_LIMIT = 7.0
DEPTH = 1
DEEPNORM_ALPHA = (2.0 * DEPTH) ** 0.25
LN_EPS = 1e-5
RMS_EPS = 1e-6
LOG2_E = 1.4426950408889634

LANES = 128
BF16_SUBLANES = 16
PAGES_PER_STEP = 8
VMEM_LIMIT = 56 * 1024 * 1024

_NT = (((1,), (1,)), ((), ()))


def _dot(a, b):
    return jnp.dot(a, b, preferred_element_type=F32)


def _dot_nt(a, b):
    return lax.dot_general(a, b, _NT, preferred_element_type=F32)


def _split3(x):
    hi = x.astype(BF16)
    r = x - hi.astype(F32)
    mid = r.astype(BF16)
    lo = (r - mid.astype(F32)).astype(BF16)
    return hi, mid, lo


def _dot_f32_lhs(x, w01):
    m = x.shape[0]
    pad = -m % BF16_SUBLANES
    if pad:
        x = jnp.concatenate([x, jnp.zeros((pad, x.shape[1]), F32)], axis=0)
    hi, mid, lo = _split3(x)
    return (_dot(lo, w01) + _dot(mid, w01) + _dot(hi, w01))[:m]


def _dot_f32_rhs(w01, x):
    hi, mid, lo = _split3(x)
    return _dot(w01, lo) + _dot(w01, mid) + _dot(w01, hi)


def _sigmoid(x):
    return 1.0 / (1.0 + jnp.exp(-x))


def _log_sigmoid(x):
    return jnp.minimum(x, 0.0) - jnp.log1p(jnp.exp(-jnp.abs(x)))


def _iota(shape, dim):
    return lax.broadcasted_iota(jnp.int32, shape, dim)


def _params(sem):
    return pltpu.CompilerParams(dimension_semantics=sem, vmem_limit_bytes=VMEM_LIMIT)


_IN_BLOCK = 512
_J_FQ, _J_FK, _J_FV, _J_GQ, _J_GV, _J_GG, _J_GA, _J_GB, _J_END = 0, 1, 2, 3, 4, 6, 8, 10, 12


def _inproj_kernel(x_ref, w_ref, wf_ref, bf_ref, wlr_ref, wau_ref, ba_ref, wkt_ref, wlrt_ref, waut_ref, bat_ref,
                   fq_ref, fk_ref, fv_ref, fkb_ref, fvb_ref, gq_ref, gv_ref, gg_ref, ga_ref, gb_ref,
                   logf_ref, logfw_ref, la_ref, gkt_ref, lat_ref, xb_ref):
    j = pl.program_id(1)

    @pl.when(j == 0)
    def _():
        xb = x_ref[...].astype(BF16)
        xb_ref[...] = xb
        logf = _log_sigmoid(_dot(xb, wf_ref[...]) + bf_ref[...])
        logf_ref[...] = logf[:, :FOX_HEADS]
        logfw_ref[...] = jnp.where(_iota(logf.shape, 1) < FOX_HEADS, logf, 0.0)
        glr = _dot(xb, wlr_ref[...]).astype(BF16)
        la_ref[...] = _log_sigmoid(_dot(glr, wau_ref[...]) + ba_ref[...]) * (1.0 / GLA_TAU)
        gkt_ref[0] = _dot_nt(wkt_ref[...], xb)
        glrt = _dot_nt(wlrt_ref[...], xb).astype(BF16)
        lat_ref[0] = _log_sigmoid(_dot(waut_ref[...], glrt) + bat_ref[...]) * (1.0 / GLA_TAU)

    r = _dot(xb_ref[...], w_ref[...])

    @pl.when(j == _J_FQ)
    def _():
        fq_ref[...] = (r * (FOX_HEAD_DIM ** -0.5 * LOG2_E)).astype(BF16)

    @pl.when(j == _J_FK)
    def _():
        fk_ref[...] = r
        fkb_ref[...] = r.astype(BF16)

    @pl.when(j == _J_FV)
    def _():
        fv_ref[...] = r
        fvb_ref[...] = r.astype(BF16)

    @pl.when(j == _J_GQ)
    def _():
        gq_ref[...] = r * GLA_DK ** -0.5

    @pl.when((j >= _J_GV) & (j < _J_GG))
    def _():
        gv_ref[...] = r.astype(BF16)

    @pl.when((j >= _J_GG) & (j < _J_GA))
    def _():
        gg_ref[...] = r * _sigmoid(r)

    @pl.when((j >= _J_GA) & (j < _J_GB))
    def _():
        ga_ref[...] = _sigmoid(r)

    @pl.when(j >= _J_GB)
    def _():
        gb_ref[...] = _sigmoid(r)


def _in_projection(x2d, wp, n_seq):
    t, d = x2d.shape
    tm = min(_IN_BLOCK, t)
    seq = t // n_seq
    assert t % tm == 0 and seq % tm == 0 or n_seq == 1
    tiles_per_seq = max(seq // tm, 1)
    nb = _IN_BLOCK

    def col(start, n):
        return lambda i, j: (i, jnp.clip(j - start, 0, n - 1))

    def whole(shape):
        return pl.BlockSpec(shape, lambda i, j: (0,) * len(shape))

    def tmap(i, j):
        return (i // tiles_per_seq, 0, i % tiles_per_seq)

    out_shape = (
        jax.ShapeDtypeStruct((t, FOX_WIDTH), BF16),
        jax.ShapeDtypeStruct((t, FOX_WIDTH), F32),
        jax.ShapeDtypeStruct((t, FOX_WIDTH), F32),
        jax.ShapeDtypeStruct((t, FOX_WIDTH), BF16),
        jax.ShapeDtypeStruct((t, FOX_WIDTH), BF16),
        jax.ShapeDtypeStruct((t, GLA_QK_WIDTH), F32),
        jax.ShapeDtypeStruct((t, GLA_V_WIDTH), BF16),
        jax.ShapeDtypeStruct((t, GLA_V_WIDTH), F32),
        jax.ShapeDtypeStruct((t, d), F32),
        jax.ShapeDtypeStruct((t, d), F32),
        jax.ShapeDtypeStruct((t, FOX_HEADS), F32),
        jax.ShapeDtypeStruct((t, LANES), F32),
        jax.ShapeDtypeStruct((t, GLA_QK_WIDTH), F32),
        jax.ShapeDtypeStruct((n_seq, GLA_QK_WIDTH, seq), F32),
        jax.ShapeDtypeStruct((n_seq, GLA_QK_WIDTH, seq), F32),
    )
    out_specs = (
        pl.BlockSpec((tm, nb), col(_J_FQ, 1)),
        pl.BlockSpec((tm, nb), col(_J_FK, 1)),
        pl.BlockSpec((tm, nb), col(_J_FV, 1)),
        pl.BlockSpec((tm, nb), col(_J_FK, 1)),
        pl.BlockSpec((tm, nb), col(_J_FV, 1)),
        pl.BlockSpec((tm, nb), col(_J_GQ, 1)),
        pl.BlockSpec((tm, nb), col(_J_GV, 2)),
        pl.BlockSpec((tm, nb), col(_J_GG, 2)),
        pl.BlockSpec((tm, nb), col(_J_GA, 2)),
        pl.BlockSpec((tm, nb), col(_J_GB, 2)),
        pl.BlockSpec((tm, FOX_HEADS), lambda i, j: (i, 0)),
        pl.BlockSpec((tm, LANES), lambda i, j: (i, 0)),
        pl.BlockSpec((tm, GLA_QK_WIDTH), lambda i, j: (i, 0)),
        pl.BlockSpec((1, GLA_QK_WIDTH, tm), tmap),
        pl.BlockSpec((1, GLA_QK_WIDTH, tm), tmap),
    )
    in_specs = [
        pl.BlockSpec((tm, d), lambda i, j: (i, 0)),
        pl.BlockSpec((d, nb), lambda i, j: (0, j)),
        whole((d, LANES)), whole((1, LANES)), whole((d, LANES)), whole((LANES, GLA_QK_WIDTH)),
        whole((1, GLA_QK_WIDTH)), whole((GLA_QK_WIDTH, d)), whole((LANES, d)), whole((GLA_QK_WIDTH, LANES)),
        whole((GLA_QK_WIDTH, 1)),
    ]
    return pl.pallas_call(
        _inproj_kernel,
        out_shape=out_shape,
        grid=(t // tm, _J_END),
        in_specs=in_specs,
        out_specs=out_specs,
        scratch_shapes=[pltpu.VMEM((tm, d), BF16)],
        compiler_params=_params(("parallel", "arbitrary")),
        name="in_projection",
    )(x2d, wp["w_main"], wp["w_ff"], wp["b_f"], wp["w_lr"], wp["w_au"], wp["b_a"],
      wp["w_gk_t"], wp["w_lr_t"], wp["w_au_t"], wp["b_a_t"])


_SLOT = LANES
_SLOT_WIDTH = FOX_HEADS * _SLOT
_AUX = FOX_HEAD_DIM
_PREP_BLOCK = 512


def _slot_layout():
    n_in = FOX_WIDTH + 3 * LANES
    pq = np.zeros((n_in, _SLOT_WIDTH), np.float32)
    pk = np.zeros((n_in, _SLOT_WIDTH), np.float32)
    pv = np.zeros((FOX_WIDTH, _SLOT_WIDTH), np.float32)
    one_q = np.zeros((1, _SLOT_WIDTH), np.float32)
    one_k = np.zeros((1, _SLOT_WIDTH), np.float32)
    one_v = np.zeros((1, _SLOT_WIDTH), np.float32)
    for h in range(FOX_HEADS):
        for dd in range(FOX_HEAD_DIM):
            pq[h * FOX_HEAD_DIM + dd, h * _SLOT + dd] = 1.0
            pk[h * FOX_HEAD_DIM + dd, h * _SLOT + dd] = 1.0
            pv[h * FOX_HEAD_DIM + dd, h * _SLOT + dd] = 1.0
        for piece in range(3):
            pq[FOX_WIDTH + piece * LANES + h, h * _SLOT + _AUX + piece] = 1.0
            one_k[0, h * _SLOT + _AUX + piece] = 1.0
            pk[FOX_WIDTH + piece * LANES + h, h * _SLOT + _AUX + 3 + piece] = -1.0
            one_q[0, h * _SLOT + _AUX + 3 + piece] = 1.0
        one_v[0, h * _SLOT + _AUX] = 1.0
    as_bf = lambda a: jnp.asarray(a, BF16)
    return as_bf(pq), as_bf(pk), as_bf(pv), jnp.asarray(one_q), jnp.asarray(one_k), jnp.asarray(one_v)


def _fox_prep_kernel(q_ref, k_ref, v_ref, lf_ref, pq_ref, pk_ref, pv_ref, oq_ref, ok_ref, ov_ref,
                     qs_ref, ks_ref, vs_ref, carry_ref):
    @pl.when(pl.program_id(1) == 0)
    def _():
        carry_ref[...] = jnp.zeros(carry_ref.shape, F32)

    rows = q_ref.shape[0]
    lower = (_iota((LANES, LANES), 0) >= _iota((LANES, LANES), 1)).astype(BF16)
    for ci in range(rows // LANES):
        rs = slice(ci * LANES, (ci + 1) * LANES)
        c = _dot_f32_rhs(lower, lf_ref[rs, :]) + carry_ref[...]
        carry_ref[...] = c[LANES - 1:LANES, :]
        pieces = list(_split3(c * LOG2_E))
        qs_ref[rs, :] = (_dot(jnp.concatenate([q_ref[rs, :]] + pieces, axis=1), pq_ref[...])
                         + oq_ref[...]).astype(BF16)
        ks_ref[rs, :] = (_dot(jnp.concatenate([k_ref[rs, :]] + pieces, axis=1), pk_ref[...])
                         + ok_ref[...]).astype(BF16)
        vs_ref[rs, :] = (_dot(v_ref[rs, :], pv_ref[...]) + ov_ref[...]).astype(BF16)


def _fox_prep(fq, fkb, fvb, logf_wide, n, s_len):
    tm = min(_PREP_BLOCK, s_len)
    steps = s_len // tm
    pq, pk, pv, one_q, one_k, one_v = _slot_layout()
    tok = lambda w: pl.BlockSpec((tm, w), lambda b, t: (b * steps + t, 0))
    whole = lambda a: pl.BlockSpec(a.shape, lambda b, t: (0, 0))
    out = jax.ShapeDtypeStruct((n * s_len, _SLOT_WIDTH), BF16)
    return pl.pallas_call(
        _fox_prep_kernel,
        out_shape=(out, out, out),
        grid=(n, steps),
        in_specs=[tok(FOX_WIDTH), tok(FOX_WIDTH), tok(FOX_WIDTH), tok(LANES),
                  whole(pq), whole(pk), whole(pv), whole(one_q), whole(one_k), whole(one_v)],
        out_specs=(tok(_SLOT_WIDTH), tok(_SLOT_WIDTH), tok(_SLOT_WIDTH)),
        scratch_shapes=[pltpu.VMEM((1, LANES), F32)],
        compiler_params=_params(("parallel", "arbitrary")),
        name="fox_prep",
    )(fq, fkb, fvb, logf_wide, pq, pk, pv, one_q, one_k, one_v)


_FOX_BLOCK = 512


def _fox_prompt_kernel(qi_tab, ki_tab, q_ref, k_ref, v_ref, o_ref, m_ref, acc_ref):
    step = pl.program_id(1)
    qi = qi_tab[step]
    ki = ki_tab[step]
    tq, tk = q_ref.shape[0], k_ref.shape[0]

    @pl.when(ki == 0)
    def _():
        m_ref[...] = jnp.full(m_ref.shape, -1e30, F32)
        acc_ref[...] = jnp.zeros(acc_ref.shape, F32)

    def sweep(diagonal):
        for h in range(FOX_HEADS):
            hs = slice(h * _SLOT, (h + 1) * _SLOT)
            s = _dot_nt(q_ref[:, hs], k_ref[:, hs])
            if diagonal:
                s = jnp.where(_iota((tq, tk), 1) <= _iota((tq, tk), 0), s, -jnp.inf)
            m_prev = m_ref[h]
            m_new = jnp.maximum(m_prev, jnp.max(s, axis=1, keepdims=True))
            p = jnp.exp2(s - m_new)
            m_ref[h] = m_new
            acc_ref[h] = acc_ref[h] * jnp.exp2(m_prev - m_new) + _dot(p.astype(BF16), v_ref[:, hs])

    @pl.when(ki < qi)
    def _():
        sweep(False)

    @pl.when(ki == qi)
    def _():
        sweep(True)
        for h in range(FOX_HEADS):
            a = acc_ref[h]
            o_ref[:, h * _SLOT:(h + 1) * _SLOT] = (a / a[:, _AUX:_AUX + 1]).astype(o_ref.dtype)


def _fox_prompt(q_slots, k_slots, v_slots, n, s_len):
    tq = min(_FOX_BLOCK, s_len)
    nq = s_len // tq
    pairs = [(a, b) for a in range(nq) for b in range(a + 1)]
    qi_tab = jnp.asarray(np.array([p[0] for p in pairs], np.int32))
    ki_tab = jnp.asarray(np.array([p[1] for p in pairs], np.int32))
    grid_spec = pltpu.PrefetchScalarGridSpec(
        num_scalar_prefetch=2,
        grid=(n, len(pairs)),
        in_specs=[
            pl.BlockSpec((tq, _SLOT_WIDTH), lambda b, t, qt, kt: (b * nq + qt[t], 0)),
            pl.BlockSpec((tq, _SLOT_WIDTH), lambda b, t, qt, kt: (b * nq + kt[t], 0)),
            pl.BlockSpec((tq, _SLOT_WIDTH), lambda b, t, qt, kt: (b * nq + kt[t], 0)),
        ],
        out_specs=pl.BlockSpec((tq, _SLOT_WIDTH), lambda b, t, qt, kt: (b * nq + qt[t], 0)),
        scratch_shapes=[
            pltpu.VMEM((FOX_HEADS, tq, 1), F32),
            pltpu.VMEM((FOX_HEADS, tq, _SLOT), F32),
        ],
    )
    return pl.pallas_call(
        _fox_prompt_kernel,
        out_shape=jax.ShapeDtypeStruct((n * s_len, _SLOT_WIDTH), BF16),
        grid_spec=grid_spec,
        compiler_params=_params(("parallel", "arbitrary")),
        name="fox_prompt",
    )(qi_tab, ki_tab, q_slots, k_slots, v_slots)


def _fox_sample_kernel(pt_ref, q_ref, kn_ref, vn_ref, lfn_ref, *refs, n_new, page):
    del pt_ref
    pp = PAGES_PER_STEP
    k_refs, v_refs, lf_refs = refs[:pp], refs[pp:2 * pp], refs[2 * pp:3 * pp]
    o_ref, carry_ref, m_ref, l_ref, acc_ref, kbf_ref, vbf_ref = refs[3 * pp:]
    step = pl.program_id(1)
    rows = FOX_HEADS * n_new
    page_rows = page * FOX_HEADS
    tiles = page_rows // LANES
    lane_i, lane_j = _iota((LANES, LANES), 0), _iota((LANES, LANES), 1)
    same_head = (lane_i % FOX_HEADS) == (lane_j % FOX_HEADS)
    own = (_iota((rows, LANES), 1) % FOX_HEADS) == (_iota((rows, LANES), 0) // n_new)

    def attend(s, vals):
        m_prev = m_ref[...]
        m_new = jnp.maximum(m_prev, jnp.max(s, axis=1, keepdims=True))
        p = jnp.exp2(s - m_new)
        corr = jnp.exp2(m_prev - m_new)
        l_ref[...] = corr * l_ref[...] + jnp.sum(p, axis=1, keepdims=True)
        m_ref[...] = m_new
        acc_ref[...] = acc_ref[...] * corr + _dot(p.astype(BF16), vals)

    @pl.when(step == 0)
    def _():
        carry_ref[...] = jnp.zeros(carry_ref.shape, F32)
        m_ref[...] = jnp.full(m_ref.shape, -1e30, F32)
        l_ref[...] = jnp.zeros(l_ref.shape, F32)
        acc_ref[...] = jnp.zeros(acc_ref.shape, F32)
        upto = (same_head & (lane_i <= lane_j)).astype(BF16)
        cn = _dot_f32_lhs(lfn_ref[0], upto) * LOG2_E
        s = _dot_nt(q_ref[0], kn_ref[0]) - cn
        visible = own & ((_iota((rows, LANES), 1) // FOX_HEADS) <= (_iota((rows, LANES), 0) % n_new))
        attend(jnp.where(visible, s, -jnp.inf), vn_ref[0])

    for i in range(pp):
        kbf_ref[i * page_rows:(i + 1) * page_rows] = k_refs[i][0].reshape(page_rows, FOX_HEAD_DIM).astype(BF16)
        vbf_ref[i * page_rows:(i + 1) * page_rows] = v_refs[i][0].reshape(page_rows, FOX_HEAD_DIM).astype(BF16)
    lf = jnp.concatenate([r[0] for r in lf_refs], axis=0)
    nt = pp * tiles
    ti, tj = _iota((nt, nt), 0), _iota((nt, nt), 1)
    tile_later = (((tj // tiles) < (ti // tiles)) | (((tj // tiles) == (ti // tiles)) & (tj > ti))).astype(BF16)
    in_tile = _dot_f32_lhs(lf, (same_head & (lane_i > lane_j)).astype(BF16))
    heads_all = same_head.astype(BF16)
    from_tiles = _dot_f32_lhs(_dot_f32_rhs(tile_later, lf), heads_all)
    bias = (in_tile + from_tiles + carry_ref[...]) * LOG2_E
    carry_ref[...] = carry_ref[...] + _dot_f32_lhs(jnp.sum(lf, axis=0, keepdims=True), heads_all)
    bias_row = jnp.concatenate([bias[c:c + 1] for c in range(nt)], axis=1)
    not_own = jnp.where(own, 0.0, -jnp.inf)
    s = _dot_nt(q_ref[0], kbf_ref[...]) + bias_row + jnp.concatenate([not_own] * nt, axis=1)
    attend(s, vbf_ref[...])

    @pl.when(step == pl.num_programs(1) - 1)
    def _():
        o_ref[0] = (acc_ref[...] / l_ref[...]).astype(o_ref.dtype)


def _fox_sample(q_rows, k_new, v_new, logf_new, cache_k, cache_v, cache_logf, page_table, n_new):
    b, n_pages = page_table.shape
    page = cache_k.shape[1]
    pp = PAGES_PER_STEP
    rows = FOX_HEADS * n_new
    page_rows = page * FOX_HEADS
    assert n_pages % pp == 0 and page_rows % LANES == 0 and rows <= LANES

    def page_spec(shape, i):
        return pl.BlockSpec(shape, lambda s, t, pt: (pt[s, n_pages - 1 - (t * pp + i)],) + (0,) * (len(shape) - 1))

    seq = lambda shape: pl.BlockSpec(shape, lambda s, t, pt: (s, 0, 0))
    grid_spec = pltpu.PrefetchScalarGridSpec(
        num_scalar_prefetch=1,
        grid=(b, n_pages // pp),
        in_specs=([seq((1, rows, FOX_HEAD_DIM)), seq((1, LANES, FOX_HEAD_DIM)), seq((1, LANES, FOX_HEAD_DIM)),
                   seq((1, 1, LANES))]
                  + [page_spec((1, page, FOX_HEADS, FOX_HEAD_DIM), i) for i in range(pp)]
                  + [page_spec((1, page, FOX_HEADS, FOX_HEAD_DIM), i) for i in range(pp)]
                  + [page_spec((1, page_rows // LANES, LANES), i) for i in range(pp)]),
        out_specs=seq((1, rows, FOX_HEAD_DIM)),
        scratch_shapes=[
            pltpu.VMEM((1, LANES), F32),
            pltpu.VMEM((rows, 1), F32),
            pltpu.VMEM((rows, 1), F32),
            pltpu.VMEM((rows, FOX_HEAD_DIM), F32),
            pltpu.VMEM((pp * page_rows, FOX_HEAD_DIM), BF16),
            pltpu.VMEM((pp * page_rows, FOX_HEAD_DIM), BF16),
        ],
    )
    return pl.pallas_call(
        functools.partial(_fox_sample_kernel, n_new=n_new, page=page),
        out_shape=jax.ShapeDtypeStruct((b, rows, FOX_HEAD_DIM), BF16),
        grid_spec=grid_spec,
        compiler_params=_params(("parallel", "arbitrary")),
        name="fox_sample",
    )(page_table, q_rows, k_new, v_new, logf_new, *([cache_k] * pp), *([cache_v] * pp), *([cache_logf] * pp))


_GLA_CHUNKS_PER_STEP = 4


def _gla_kernel(q_ref, la_ref, v_ref, gg_ref, kt_ref, lat_ref, gn_ref, s0_ref, og_ref, sfin_ref, st_ref, *, cps):
    c = GLA_CHUNK
    step = pl.program_id(1)

    @pl.when(step == 0)
    def _():
        st_ref[...] = s0_ref[0]

    lower = (_iota((c, c), 0) >= _iota((c, c), 1))
    lower_b = lower.astype(BF16)
    upper_b = (_iota((c, c), 0) <= _iota((c, c), 1)).astype(BF16)
    for ci in range(cps):
        rs = slice(ci * c, (ci + 1) * c)
        for h in range(GLA_HEADS):
            ks = slice(h * GLA_DK, (h + 1) * GLA_DK)
            vs = slice(h * GLA_DV, (h + 1) * GLA_DV)
            b = _dot_f32_rhs(lower_b, la_ref[rs, ks])
            bt = _dot_f32_lhs(lat_ref[0, ks, rs], upper_b)
            b_last = bt[:, c - 1:c]
            kt = kt_ref[0, ks, rs]
            q_dec = (q_ref[rs, ks] * jnp.exp(b)).astype(BF16)
            k_dec_t = (kt * jnp.exp(-bt)).astype(BF16)
            k_end_t = (kt * jnp.exp(b_last - bt)).astype(BF16)
            v = v_ref[rs, vs]
            attn = jnp.where(lower, _dot(q_dec, k_dec_t), 0.0)
            state = st_ref[h]
            o = _dot(attn.astype(BF16), v) + _dot(q_dec, state.astype(BF16))
            st_ref[h] = state * jnp.exp(b_last) + _dot(k_end_t, v)
            scale = lax.rsqrt(jnp.mean(jnp.square(o), axis=1, keepdims=True) + RMS_EPS)
            og_ref[rs, vs] = (o * scale * gn_ref[:, vs] * gg_ref[rs, vs]).astype(og_ref.dtype)

    @pl.when(step == pl.num_programs(1) - 1)
    def _():
        sfin_ref[0] = st_ref[...]


def _gla(gq, la, gv, gg, gk_t, la_t, gla_norm, s0, n_seq, seq):
    cps = min(_GLA_CHUNKS_PER_STEP, seq // GLA_CHUNK)
    ts = cps * GLA_CHUNK
    steps = seq // ts
    tok = lambda s, t: (s * steps + t, 0)
    return pl.pallas_call(
        functools.partial(_gla_kernel, cps=cps),
        out_shape=(jax.ShapeDtypeStruct((n_seq * seq, GLA_V_WIDTH), BF16),
                   jax.ShapeDtypeStruct((n_seq, GLA_HEADS, GLA_DK, GLA_DV), F32)),
        grid=(n_seq, steps),
        in_specs=[
            pl.BlockSpec((ts, GLA_QK_WIDTH), tok),
            pl.BlockSpec((ts, GLA_QK_WIDTH), tok),
            pl.BlockSpec((ts, GLA_V_WIDTH), tok),
            pl.BlockSpec((ts, GLA_V_WIDTH), tok),
            pl.BlockSpec((1, GLA_QK_WIDTH, ts), lambda s, t: (s, 0, t)),
            pl.BlockSpec((1, GLA_QK_WIDTH, ts), lambda s, t: (s, 0, t)),
            pl.BlockSpec((1, GLA_V_WIDTH), lambda s, t: (0, 0)),
            pl.BlockSpec((1, GLA_HEADS, GLA_DK, GLA_DV), lambda s, t: (s, 0, 0, 0)),
        ],
        out_specs=(pl.BlockSpec((ts, GLA_V_WIDTH), tok),
                   pl.BlockSpec((1, GLA_HEADS, GLA_DK, GLA_DV), lambda s, t: (s, 0, 0, 0))),
        scratch_shapes=[pltpu.VMEM((GLA_HEADS, GLA_DK, GLA_DV), F32)],
        compiler_params=_params(("parallel", "arbitrary")),
        name="gla",
    )(gq, la, gv, gg, gk_t, la_t, gla_norm, s0)


_MIX_BLOCK = 512


def _pack_bf16_pairs(x):
    w = x.shape[1] // 2
    bits = lax.bitcast_convert_type(x.astype(BF16).astype(F32), jnp.uint32)
    return (bits[:, :w] >> 16) | (bits[:, w:] & jnp.uint32(0xFFFF0000))


def _unpack_bf16_pairs(u):
    lo = lax.bitcast_convert_type(u << 16, F32).astype(BF16)
    hi = lax.bitcast_convert_type(u & jnp.uint32(0xFFFF0000), F32).astype(BF16)
    return lo, hi


def _layer_norm(z, g, b):
    mu = jnp.mean(z, axis=1, keepdims=True)
    zc = z - mu
    var = jnp.mean(jnp.square(zc), axis=1, keepdims=True)
    return zc * lax.rsqrt(var + LN_EPS) * g + b


def _mix_kernel(of_ref, og_ref, ga_ref, gb_ref, x_ref, p_ref, wof_ref, wog_ref, wout_ref, wpg_ref, bpg_ref,
                wpp_ref, g1_ref, b1_ref, wr_ref, br_ref, base_ref, x1p_ref, lg_ref):
    branch_a = _dot(of_ref[...], wof_ref[...])
    branch_b = _dot(og_ref[...], wog_ref[...])
    merged = ga_ref[...] * branch_a + gb_ref[...] * branch_b
    mix = _dot(merged.astype(BF16), wout_ref[...])
    x1 = _layer_norm(DEEPNORM_ALPHA * x_ref[...] + mix, g1_ref[...], b1_ref[...])
    gate = _sigmoid(_dot(x1.astype(BF16), wpg_ref[...]) + bpg_ref[...])
    ple = gate * _dot(p_ref[...].astype(BF16), wpp_ref[...])
    base_ref[...] = DEEPNORM_ALPHA * x1 + ple
    x1p_ref[...] = _pack_bf16_pairs(x1)
    logits = jnp.dot(x1, wr_ref[...], precision=lax.Precision.HIGHEST, preferred_element_type=F32)
    lg_ref[...] = logits.T[:N_EXPERTS] + br_ref[...]


def _mix(o_fox, w_o_fox, og, gate_a, gate_b, x2d, p2d, wp):
    t, d = x2d.shape
    tm = min(_MIX_BLOCK, t)
    ple_dim = p2d.shape[1]
    fox_w = o_fox.shape[1]
    row = lambda w: pl.BlockSpec((tm, w), lambda i: (i, 0))
    whole = lambda shape: pl.BlockSpec(shape, lambda i: (0,) * len(shape))
    return pl.pallas_call(
        _mix_kernel,
        out_shape=(jax.ShapeDtypeStruct((t, d), F32),
                   jax.ShapeDtypeStruct((t, d // 2), jnp.uint32),
                   jax.ShapeDtypeStruct((N_EXPERTS, t), F32)),
        grid=(t // tm,),
        in_specs=[row(fox_w), row(GLA_V_WIDTH), row(d), row(d), row(d), row(ple_dim),
                  whole((fox_w, d)), whole((GLA_V_WIDTH, d)), whole((d, d)), whole((d, d)), whole((1, d)),
                  whole((ple_dim, d)), whole((1, d)), whole((1, d)), whole((d, LANES)), whole((N_EXPERTS, 1))],
        out_specs=(row(d), row(d // 2), pl.BlockSpec((N_EXPERTS, tm), lambda i: (0, i))),
        compiler_params=_params(("parallel",)),
        name="mix_ln_router",
    )(o_fox, og, gate_a, gate_b, x2d, p2d, w_o_fox, wp["w_o_gla"], wp["w_out"], wp["w_ple_gate"],
      wp["b_ple_gate"], wp["w_ple_proj"], wp["ln1_g"], wp["ln1_b"], wp["w_router"], wp["b_router"])


_ROUTE_BLOCK = 512


def _route_kernel(lg_ref, e_ref, w_ref, rank_ref, cnt_ref, carry_ref):
    i = pl.program_id(0)
    tn = lg_ref.shape[1]

    @pl.when(i == 0)
    def _():
        carry_ref[...] = jnp.zeros(carry_ref.shape, F32)

    lg = lg_ref[...]
    eidx = _iota((N_EXPERTS, tn), 0).astype(F32)
    vals, picks = [], []
    chosen_f = jnp.zeros((N_EXPERTS, tn), F32)
    for _ in range(TOP_K):
        mx = jnp.max(lg, axis=0, keepdims=True)
        idx = jnp.min(jnp.where(lg == mx, eidx, float(N_EXPERTS)), axis=0, keepdims=True)
        pick = eidx == idx
        vals.append(mx)
        picks.append(pick)
        chosen_f = jnp.where(pick, 1.0, chosen_f)
        lg = jnp.where(pick, -jnp.inf, lg)
    ex = [jnp.exp(v - vals[0]) for v in vals]
    den = ex[0] + ex[1] + ex[2] + ex[3]
    upper = (_iota((tn, tn), 0) <= _iota((tn, tn), 1)).astype(BF16)
    before = carry_ref[...] + _dot(chosen_f.astype(BF16), upper) - chosen_f
    for k in range(TOP_K):
        e_ref[k:k + 1, :] = jnp.sum(jnp.where(picks[k], eidx, 0.0), axis=0, keepdims=True).astype(jnp.int32)
        w_ref[k:k + 1, :] = ex[k] / den
        rank_ref[k:k + 1, :] = jnp.sum(jnp.where(picks[k], before, 0.0), axis=0, keepdims=True).astype(jnp.int32)
    carry_ref[...] = carry_ref[...] + jnp.sum(chosen_f, axis=1, keepdims=True)
    cnt_ref[...] = jnp.broadcast_to(carry_ref[...], cnt_ref.shape)


def _route(logits_t):
    t = logits_t.shape[1]
    tn = min(_ROUTE_BLOCK, t)
    blk = pl.BlockSpec((TOP_K, tn), lambda i: (0, i))
    return pl.pallas_call(
        _route_kernel,
        out_shape=(jax.ShapeDtypeStruct((TOP_K, t), jnp.int32),
                   jax.ShapeDtypeStruct((TOP_K, t), F32),
                   jax.ShapeDtypeStruct((TOP_K, t), jnp.int32),
                   jax.ShapeDtypeStruct((N_EXPERTS, LANES), F32)),
        grid=(t // tn,),
        in_specs=[pl.BlockSpec((N_EXPERTS, tn), lambda i: (0, i))],
        out_specs=(blk, blk, blk, pl.BlockSpec((N_EXPERTS, LANES), lambda i: (0, 0))),
        scratch_shapes=[pltpu.VMEM((N_EXPERTS, 1), F32)],
        compiler_params=_params(("arbitrary",)),
        name="moe_route",
    )(logits_t)


def _slots_kernel(e_ref, rank_ref, cnt_ref, dest_ref, be_ref, *, block_rows):
    tn = e_ref.shape[1]
    cnt = cnt_ref[...]
    padded = jnp.ceil(cnt * (1.0 / block_rows)) * block_rows
    incl = (_iota((N_EXPERTS, N_EXPERTS), 0) >= _iota((N_EXPERTS, N_EXPERTS), 1)).astype(BF16)
    ends = _dot_f32_rhs(incl, padded)
    starts = (ends - padded)[:, 0:1]
    eidx = _iota((N_EXPERTS, tn), 0)
    for k in range(TOP_K):
        first = jnp.sum(jnp.where(eidx == e_ref[k:k + 1, :], starts, 0.0), axis=0, keepdims=True)
        dest_ref[k:k + 1, :] = first.astype(jnp.int32) + rank_ref[k:k + 1, :]
    nb = be_ref.shape[1]
    row_start = (_iota((N_EXPERTS, nb), 1) * block_rows).astype(F32)
    owner = jnp.sum(jnp.where(ends[:, 0:1] <= row_start, 1.0, 0.0), axis=0, keepdims=True)
    be_ref[...] = jnp.minimum(owner, N_EXPERTS - 1.0).astype(jnp.int32)


def _slots(experts, ranks, counts, block_rows, n_blocks):
    t = experts.shape[1]
    tn = min(_ROUTE_BLOCK, t)
    nb_pad = -(-n_blocks // LANES) * LANES
    blk = pl.BlockSpec((TOP_K, tn), lambda i: (0, i))
    return pl.pallas_call(
        functools.partial(_slots_kernel, block_rows=block_rows),
        out_shape=(jax.ShapeDtypeStruct((TOP_K, t), jnp.int32),
                   jax.ShapeDtypeStruct((1, nb_pad), jnp.int32)),
        grid=(t // tn,),
        in_specs=[blk, blk, pl.BlockSpec((N_EXPERTS, LANES), lambda i: (0, 0))],
        out_specs=(blk, pl.BlockSpec((1, nb_pad), lambda i: (0, 0))),
        compiler_params=_params(("arbitrary",)),
        name="moe_slots",
    )(experts, ranks, counts)


_DISPATCH_BLOCK = 512


def _dispatch_kernel(dest_ref, x_ref, zero_ref, xb_ref, sem):
    del zero_ref
    tn = dest_ref.shape[1]

    def issue(j, carry):
        for k in range(TOP_K):
            pltpu.make_async_copy(x_ref.at[pl.ds(j, 1)], xb_ref.at[pl.ds(dest_ref[k, j], 1)], sem).start(priority=k % 2)
        return carry

    lax.fori_loop(0, tn, issue, 0)
    for k in range(TOP_K):
        pltpu.make_async_copy(x_ref, xb_ref.at[pl.ds(0, tn)], sem).wait()


def _dispatch(x_packed, dest, n_slots):
    t, w = x_packed.shape
    tn = min(_DISPATCH_BLOCK, t)
    zeros = jnp.zeros((n_slots, w), x_packed.dtype)
    return pl.pallas_call(
        _dispatch_kernel,
        out_shape=jax.ShapeDtypeStruct((n_slots, w), x_packed.dtype),
        grid=(t // tn,),
        in_specs=[pl.BlockSpec((TOP_K, tn), lambda i: (0, i), memory_space=pltpu.SMEM),
                  pl.BlockSpec((tn, w), lambda i: (i, 0)),
                  pl.BlockSpec(memory_space=pl.ANY)],
        out_specs=pl.BlockSpec(memory_space=pl.ANY),
        scratch_shapes=[pltpu.SemaphoreType.DMA(())],
        input_output_aliases={2: 0},
        compiler_params=_params(("arbitrary",)),
        name="moe_dispatch",
    )(dest, x_packed, zeros)


def _expert_kernel(be_ref, x_ref, wgu_ref, bgu_ref, wd_ref, bd_ref, y_ref):
    del be_ref
    lo, hi = _unpack_bf16_pairs(x_ref[...])
    half = lo.shape[1]
    h = _dot(lo, wgu_ref[0, :half]) + _dot(hi, wgu_ref[0, half:]) + bgu_ref[0]
    d_ff = h.shape[1] // 2
    gate = jnp.minimum(h[:, :d_ff], SWIGLU_LIMIT)
    up = jnp.clip(h[:, d_ff:], -SWIGLU_LIMIT, SWIGLU_LIMIT)
    hid = (up + 1.0) * gate * _sigmoid(SWIGLU_ALPHA * gate)
    y_ref[...] = _dot(hid.astype(BF16), wd_ref[0]) + bd_ref[0]


def _experts(xb, block_expert, w_gu, b_gu, w_down, b_down, block_rows, n_blocks):
    d = w_down.shape[2]
    wexp = lambda *shape: pl.BlockSpec((1,) + shape, lambda i, be: (be[i],) + (0,) * len(shape))
    grid_spec = pltpu.PrefetchScalarGridSpec(
        num_scalar_prefetch=1,
        grid=(n_blocks,),
        in_specs=[pl.BlockSpec((block_rows, xb.shape[1]), lambda i, be: (i, 0)),
                  wexp(*w_gu.shape[1:]), wexp(*b_gu.shape[1:]), wexp(*w_down.shape[1:]), wexp(*b_down.shape[1:])],
        out_specs=pl.BlockSpec((block_rows, d), lambda i, be: (i, 0)),
    )
    return pl.pallas_call(
        _expert_kernel,
        out_shape=jax.ShapeDtypeStruct((n_blocks * block_rows, d), F32),
        grid_spec=grid_spec,
        compiler_params=_params(("parallel",)),
        name="moe_experts",
    )(block_expert, xb, w_gu, b_gu, w_down, b_down)


_COMBINE_BLOCK = 256


def _combine_kernel(dest_ref, yb_ref, w_ref, base_ref, g_ref, b_ref, o_ref, buf_ref, sem):
    tm = base_ref.shape[0]

    def issue(j, carry):
        for k in range(TOP_K):
            pltpu.make_async_copy(yb_ref.at[pl.ds(dest_ref[k, j], 1)], buf_ref.at[k, pl.ds(j, 1)],
                                  sem).start(priority=k % 2)
        return carry

    lax.fori_loop(0, tm, issue, 0)
    for k in range(TOP_K):
        pltpu.make_async_copy(yb_ref.at[pl.ds(0, tm)], buf_ref.at[k], sem).wait()
    y = base_ref[...]
    for k in range(TOP_K):
        y = y + buf_ref[k] * w_ref[:, k:k + 1]
    o_ref[...] = _layer_norm(y, g_ref[...], b_ref[...])


def _combine(yb, dest, w_tok, base, ln_g, ln_b):
    t, d = base.shape
    tm = min(_COMBINE_BLOCK, t)
    return pl.pallas_call(
        _combine_kernel,
        out_shape=jax.ShapeDtypeStruct((t, d), F32),
        grid=(t // tm,),
        in_specs=[pl.BlockSpec((TOP_K, tm), lambda i: (0, i), memory_space=pltpu.SMEM),
                  pl.BlockSpec(memory_space=pl.ANY),
                  pl.BlockSpec((tm, TOP_K), lambda i: (i, 0)),
                  pl.BlockSpec((tm, d), lambda i: (i, 0)),
                  pl.BlockSpec((1, d), lambda i: (0, 0)),
                  pl.BlockSpec((1, d), lambda i: (0, 0))],
        out_specs=pl.BlockSpec((tm, d), lambda i: (i, 0)),
        scratch_shapes=[pltpu.VMEM((TOP_K, tm, d), F32), pltpu.SemaphoreType.DMA(())],
        compiler_params=_params(("arbitrary",)),
        name="moe_combine",
    )(dest, yb, w_tok, base, ln_g, ln_b)


def _prepare_weights(w_in, b_f, w_a_up, b_a, gla_norm, w_o_fox, w_o_gla, w_out, ln1_g, ln1_b, w_router, b_router,
                     w_gu, b_gu, w_down, b_down, w_ple_gate, b_ple_gate, w_ple_proj, ln2_g, ln2_b):
    d = w_in.shape[0]
    widths = (FOX_WIDTH, FOX_WIDTH, FOX_WIDTH, FOX_HEADS, GLA_QK_WIDTH, GLA_QK_WIDTH, GLA_V_WIDTH, GLA_V_WIDTH,
              GLA_GATE_RANK, d, d)
    offs = np.cumsum((0,) + widths)
    fq, fk, fv, ff, gq, gk, gv, gg, glr, ga, gb = [w_in[:, offs[i]:offs[i + 1]] for i in range(len(widths))]

    def pad_cols(w, n):
        return jnp.pad(w, ((0, 0), (0, n - w.shape[1])))

    row = lambda v: v.reshape(1, -1).astype(F32)
    w_au_pad = jnp.pad(w_a_up, ((0, LANES - GLA_GATE_RANK), (0, 0)))
    return {
        "w_main": jnp.concatenate([fq, fk, fv, gq, gv, gg, ga, gb], axis=1).astype(BF16),
        "w_ff": pad_cols(ff, LANES).astype(BF16),
        "b_f": pad_cols(row(b_f), LANES),
        "w_lr": pad_cols(glr, LANES).astype(BF16),
        "w_au": w_au_pad.astype(BF16),
        "b_a": row(b_a),
        "w_gk_t": gk.T.astype(BF16),
        "w_lr_t": pad_cols(glr, LANES).T.astype(BF16),
        "w_au_t": w_au_pad.T.astype(BF16),
        "b_a_t": b_a.reshape(-1, 1).astype(F32),
        "gla_norm": row(gla_norm),
        "w_o_fox": w_o_fox.astype(BF16),
        "w_o_fox_slots": jnp.pad(w_o_fox.reshape(FOX_HEADS, FOX_HEAD_DIM, d),
                                 ((0, 0), (0, _SLOT - FOX_HEAD_DIM), (0, 0))).reshape(_SLOT_WIDTH, d).astype(BF16),
        "w_o_gla": w_o_gla.astype(BF16), "w_out": w_out.astype(BF16),
        "ln1_g": row(ln1_g), "ln1_b": row(ln1_b),
        "w_router": pad_cols(w_router, LANES).astype(F32), "b_router": b_router.reshape(-1, 1).astype(F32),
        "w_gu": w_gu.astype(BF16), "b_gu": b_gu.reshape(N_EXPERTS, 1, -1).astype(F32),
        "w_down": w_down.astype(BF16), "b_down": b_down.reshape(N_EXPERTS, 1, -1).astype(F32),
        "w_ple_gate": w_ple_gate.astype(BF16), "b_ple_gate": row(b_ple_gate),
        "w_ple_proj": w_ple_proj.astype(BF16),
        "ln2_g": row(ln2_g), "ln2_b": row(ln2_b),
    }


def _moe_and_norm(base, x1_packed, logits_t, wp, block_rows):
    t = base.shape[0]
    n_blocks = -(-(t * TOP_K) // block_rows) + N_EXPERTS
    experts, weights, ranks, counts = _route(logits_t)
    dest, block_expert = _slots(experts, ranks, counts, block_rows, n_blocks)
    xb = _dispatch(x1_packed, dest, n_blocks * block_rows)
    yb = _experts(xb, block_expert[0, :n_blocks], wp["w_gu"], wp["b_gu"], wp["w_down"], wp["b_down"],
                  block_rows, n_blocks)
    return _combine(yb, dest, weights.T, base, wp["ln2_g"], wp["ln2_b"])


def _prompt_layer(x, p, wp):
    n, s_len, d = x.shape
    x2d = x.reshape(n * s_len, d)
    fq, fk, fv, fkb, fvb, gq, gv, gg, ga, gb, logf, logf_wide, la, gk_t, la_t = _in_projection(x2d, wp, n)
    q_slots, k_slots, v_slots = _fox_prep(fq, fkb, fvb, logf_wide, n, s_len)
    o_fox = _fox_prompt(q_slots, k_slots, v_slots, n, s_len)
    s0 = jnp.zeros((n, GLA_HEADS, GLA_DK, GLA_DV), F32)
    og, s_fin = _gla(gq, la, gv, gg, gk_t, la_t, wp["gla_norm"], s0, n, s_len)
    base, x1p, logits_t = _mix(o_fox, wp["w_o_fox_slots"], og, ga, gb, x2d, p.reshape(n * s_len, -1), wp)
    y = _moe_and_norm(base, x1p, logits_t, wp, block_rows=256)
    return (y.reshape(n, s_len, d), fk.reshape(1, n, s_len, FOX_HEADS, FOX_HEAD_DIM),
            fv.reshape(1, n, s_len, FOX_HEADS, FOX_HEAD_DIM), logf.reshape(1, n, s_len, FOX_HEADS), s_fin[None])


def _sample_layer(x, p, cache_k, cache_v, cache_logf, state, page_table, wp):
    b, n_new, d = x.shape
    t = b * n_new
    x2d = x.reshape(t, d)
    fq, fk, fv, fkb, fvb, gq, gv, gg, ga, gb, logf, _, la, gk_t, la_t = _in_projection(x2d, wp, 1)
    pool, page = cache_k.shape[0], cache_k.shape[1]
    rows = FOX_HEADS * n_new
    q_rows = fq.reshape(b, n_new, FOX_HEADS, FOX_HEAD_DIM).transpose(0, 2, 1, 3).reshape(b, rows, FOX_HEAD_DIM)
    new_rows = lambda a: jnp.pad(a.reshape(b, rows, FOX_HEAD_DIM), ((0, 0), (0, LANES - rows), (0, 0)))
    logf_new = jnp.pad(logf.reshape(b, 1, rows), ((0, 0), (0, 0), (0, LANES - rows)))
    o_rows = _fox_sample(q_rows, new_rows(fkb), new_rows(fvb), logf_new, cache_k, cache_v,
                         cache_logf.reshape(pool, page * FOX_HEADS // LANES, LANES), page_table, n_new)
    o_fox = o_rows.reshape(b, FOX_HEADS, n_new, FOX_HEAD_DIM).transpose(0, 2, 1, 3).reshape(t, FOX_WIDTH)
    c = GLA_CHUNK
    pad_tok = lambda a: jnp.pad(a.reshape(b, n_new, -1), ((0, 0), (0, c - n_new), (0, 0))).reshape(b * c, -1)
    pad_time = lambda a: jnp.pad(a.reshape(-1, b, n_new).transpose(1, 0, 2), ((0, 0), (0, 0), (0, c - n_new)))
    og, s_fin = _gla(pad_tok(gq), pad_tok(la), pad_tok(gv), pad_tok(gg), pad_time(gk_t), pad_time(la_t),
                     wp["gla_norm"], state, b, c)
    og = og.reshape(b, c, -1)[:, :n_new].reshape(t, -1)
    base, x1p, logits_t = _mix(o_fox, wp["w_o_fox"], og, ga, gb, x2d, p.reshape(t, -1), wp)
    y = _moe_and_norm(base, x1p, logits_t, wp, block_rows=128)
    return (y.reshape(b, n_new, d), fk.reshape(1, b, n_new, FOX_HEADS, FOX_HEAD_DIM),
            fv.reshape(1, b, n_new, FOX_HEADS, FOX_HEAD_DIM), logf.reshape(1, b, n_new, FOX_HEADS), s_fin[None])


def kernel(x_prompt, x_sample, cache_k, cache_v, cache_logf, state_gla, page_table, p_prompt, p_sample, w_in, b_f,
           w_a_up, b_a, gla_norm, w_o_fox, w_o_gla, w_out, ln1_g, ln1_b, w_router, b_router, w_gu, b_gu, w_down,
           b_down, w_ple_gate, b_ple_gate, w_ple_proj, ln2_g, ln2_b):
    assert w_in.shape[0] == DEPTH
    wp = _prepare_weights(w_in[0], b_f[0], w_a_up[0], b_a[0], gla_norm[0], w_o_fox[0], w_o_gla[0], w_out[0],
                          ln1_g[0], ln1_b[0], w_router[0], b_router[0], w_gu[0], b_gu[0], w_down[0], b_down[0],
                          w_ple_gate[0], b_ple_gate[0], w_ple_proj[0], ln2_g[0], ln2_b[0])
    y_p, k_p, v_p, lf_p, s_p = _prompt_layer(x_prompt, p_prompt[0], wp)
    y_s, k_s, v_s, lf_s, s_s = _sample_layer(x_sample, p_sample[0], cache_k[0], cache_v[0], cache_logf[0],
                                             state_gla[0], page_table, wp)
    return (y_p, y_s, k_p, v_p, lf_p, s_p, k_s, v_s, lf_s, s_s)
```

```python
import functools

import numpy as np
import jax
import jax.numpy as jnp
from jax import lax
from jax.experimental import pallas as pl
from jax.experimental.pallas import tpu as pltpu

F32 = jnp.float32
BF16 = jnp.bfloat16

FOX_HEADS = 8
FOX_HEAD_DIM = 64
FOX_WIDTH = FOX_HEADS * FOX_HEAD_DIM
GLA_HEADS = 4
GLA_DK = 128
GLA_DV = 256
GLA_QK_WIDTH = GLA_HEADS * GLA_DK
GLA_V_WIDTH = GLA_HEADS * GLA_DV
GLA_GATE_RANK = 16
GLA_TAU = 16.0
GLA_CHUNK = 64
N_EXPERTS = 32
TOP_K = 4
SWIGLU_ALPHA = 1.702
SWIGLU_LIMIT = 7.0
DEPTH = 1
DEEPNORM_ALPHA = (2.0 * DEPTH) ** 0.25
LN_EPS = 1e-5
RMS_EPS = 1e-6
LOG2_E = 1.4426950408889634

LANES = 128
BF16_SUBLANES = 16
PAGES_PER_STEP = 16
VMEM_LIMIT = 56 * 1024 * 1024

_NT = (((1,), (1,)), ((), ()))


def _dot(a, b):
    return jnp.dot(a, b, preferred_element_type=F32)


def _dot_nt(a, b):
    return lax.dot_general(a, b, _NT, preferred_element_type=F32)


def _split3(x):
    hi = x.astype(BF16)
    r = x - hi.astype(F32)
    mid = r.astype(BF16)
    lo = (r - mid.astype(F32)).astype(BF16)
    return hi, mid, lo


def _dot_f32_lhs(x, w01):
    m = x.shape[0]
    pad = -m % BF16_SUBLANES
    if pad:
        x = jnp.concatenate([x, jnp.zeros((pad, x.shape[1]), F32)], axis=0)
    hi, mid, lo = _split3(x)
    return (_dot(lo, w01) + _dot(mid, w01) + _dot(hi, w01))[:m]


def _dot_f32_rhs(w01, x):
    hi, mid, lo = _split3(x)
    return _dot(w01, lo) + _dot(w01, mid) + _dot(w01, hi)


def _sigmoid(x):
    return 1.0 / (1.0 + jnp.exp(-x))


def _log_sigmoid(x):
    return jnp.minimum(x, 0.0) - jnp.log1p(jnp.exp(-jnp.abs(x)))


def _iota(shape, dim):
    return lax.broadcasted_iota(jnp.int32, shape, dim)


def _params(sem, flags=None):
    return pltpu.CompilerParams(dimension_semantics=sem, vmem_limit_bytes=VMEM_LIMIT, flags=flags)


_IN_BLOCK = 512
_J_FQ, _J_FK, _J_FV, _J_GQ, _J_GV, _J_GG, _J_GA, _J_GB, _J_END = 0, 1, 2, 3, 4, 6, 8, 10, 12


def _inproj_kernel(x_ref, w_ref, wf_ref, bf_ref, wlr_ref, wau_ref, ba_ref, wkt_ref, wlrt_ref, waut_ref, bat_ref,
                   fq_ref, fk_ref, fv_ref, fkb_ref, fvb_ref, gq_ref, gv_ref, gg_ref, ga_ref, gb_ref,
                   logf_ref, logfw_ref, la_ref, gkt_ref, lat_ref, xb_ref):
    j = pl.program_id(1)

    @pl.when(j == 0)
    def _():
        xb = x_ref[...].astype(BF16)
        xb_ref[...] = xb
        logf = _log_sigmoid(_dot(xb, wf_ref[...]) + bf_ref[...])
        logf_ref[...] = logf[:, :FOX_HEADS]
        logfw_ref[...] = jnp.where(_iota(logf.shape, 1) < FOX_HEADS, logf, 0.0)
        glr = _dot(xb, wlr_ref[...]).astype(BF16)
        la_ref[...] = _log_sigmoid(_dot(glr, wau_ref[...]) + ba_ref[...]) * (1.0 / GLA_TAU)
        gkt_ref[0] = _dot_nt(wkt_ref[...], xb)
        glrt = _dot_nt(wlrt_ref[...], xb).astype(BF16)
        lat_ref[0] = _log_sigmoid(_dot(waut_ref[...], glrt) + bat_ref[...]) * (1.0 / GLA_TAU)

    r = _dot(xb_ref[...], w_ref[...])

    @pl.when(j == _J_FQ)
    def _():
        fq_ref[...] = (r * (FOX_HEAD_DIM ** -0.5 * LOG2_E)).astype(BF16)

    @pl.when(j == _J_FK)
    def _():
        fk_ref[...] = r
        fkb_ref[...] = r.astype(BF16)

    @pl.when(j == _J_FV)
    def _():
        fv_ref[...] = r
        fvb_ref[...] = r.astype(BF16)

    @pl.when(j == _J_GQ)
    def _():
        gq_ref[...] = r * GLA_DK ** -0.5

    @pl.when((j >= _J_GV) & (j < _J_GG))
    def _():
        gv_ref[...] = r.astype(BF16)

    @pl.when((j >= _J_GG) & (j < _J_GA))
    def _():
        gg_ref[...] = r * _sigmoid(r)

    @pl.when((j >= _J_GA) & (j < _J_GB))
    def _():
        ga_ref[...] = _sigmoid(r)

    @pl.when(j >= _J_GB)
    def _():
        gb_ref[...] = _sigmoid(r)


def _in_projection(x2d, wp, n_seq):
    t, d = x2d.shape
    tm = min(_IN_BLOCK, t)
    seq = t // n_seq
    assert t % tm == 0 and seq % tm == 0 or n_seq == 1
    tiles_per_seq = max(seq // tm, 1)
    nb = _IN_BLOCK

    def col(start, n):
        return lambda i, j: (i, jnp.clip(j - start, 0, n - 1))

    def whole(shape):
        return pl.BlockSpec(shape, lambda i, j: (0,) * len(shape))

    def tmap(i, j):
        return (i // tiles_per_seq, 0, i % tiles_per_seq)

    out_shape = (
        jax.ShapeDtypeStruct((t, FOX_WIDTH), BF16),
        jax.ShapeDtypeStruct((t, FOX_WIDTH), F32),
        jax.ShapeDtypeStruct((t, FOX_WIDTH), F32),
        jax.ShapeDtypeStruct((t, FOX_WIDTH), BF16),
        jax.ShapeDtypeStruct((t, FOX_WIDTH), BF16),
        jax.ShapeDtypeStruct((t, GLA_QK_WIDTH), F32),
        jax.ShapeDtypeStruct((t, GLA_V_WIDTH), BF16),
        jax.ShapeDtypeStruct((t, GLA_V_WIDTH), F32),
        jax.ShapeDtypeStruct((t, d), F32),
        jax.ShapeDtypeStruct((t, d), F32),
        jax.ShapeDtypeStruct((t, FOX_HEADS), F32),
        jax.ShapeDtypeStruct((t, LANES), F32),
        jax.ShapeDtypeStruct((t, GLA_QK_WIDTH), F32),
        jax.ShapeDtypeStruct((n_seq, GLA_QK_WIDTH, seq), F32),
        jax.ShapeDtypeStruct((n_seq, GLA_QK_WIDTH, seq), F32),
    )
    out_specs = (
        pl.BlockSpec((tm, nb), col(_J_FQ, 1)),
        pl.BlockSpec((tm, nb), col(_J_FK, 1)),
        pl.BlockSpec((tm, nb), col(_J_FV, 1)),
        pl.BlockSpec((tm, nb), col(_J_FK, 1)),
        pl.BlockSpec((tm, nb), col(_J_FV, 1)),
        pl.BlockSpec((tm, nb), col(_J_GQ, 1)),
        pl.BlockSpec((tm, nb), col(_J_GV, 2)),
        pl.BlockSpec((tm, nb), col(_J_GG, 2)),
        pl.BlockSpec((tm, nb), col(_J_GA, 2)),
        pl.BlockSpec((tm, nb), col(_J_GB, 2)),
        pl.BlockSpec((tm, FOX_HEADS), lambda i, j: (i, 0)),
        pl.BlockSpec((tm, LANES), lambda i, j: (i, 0)),
        pl.BlockSpec((tm, GLA_QK_WIDTH), lambda i, j: (i, 0)),
        pl.BlockSpec((1, GLA_QK_WIDTH, tm), tmap),
        pl.BlockSpec((1, GLA_QK_WIDTH, tm), tmap),
    )
    in_specs = [
        pl.BlockSpec((tm, d), lambda i, j: (i, 0)),
        pl.BlockSpec((d, nb), lambda i, j: (0, j)),
        whole((d, LANES)), whole((1, LANES)), whole((d, LANES)), whole((LANES, GLA_QK_WIDTH)),
        whole((1, GLA_QK_WIDTH)), whole((GLA_QK_WIDTH, d)), whole((LANES, d)), whole((GLA_QK_WIDTH, LANES)),
        whole((GLA_QK_WIDTH, 1)),
    ]
    return pl.pallas_call(
        _inproj_kernel,
        out_shape=out_shape,
        grid=(t // tm, _J_END),
        in_specs=in_specs,
        out_specs=out_specs,
        scratch_shapes=[pltpu.VMEM((tm, d), BF16)],
        compiler_params=_params(("parallel", "arbitrary")),
        name="in_projection",
    )(x2d, wp["w_main"], wp["w_ff"], wp["b_f"], wp["w_lr"], wp["w_au"], wp["b_a"],
      wp["w_gk_t"], wp["w_lr_t"], wp["w_au_t"], wp["b_a_t"])


_SLOT = LANES
_SLOT_WIDTH = FOX_HEADS * _SLOT
_AUX = FOX_HEAD_DIM
_PREP_BLOCK = 512


def _slot_layout():
    n_in = FOX_WIDTH + 3 * LANES
    pq = np.zeros((n_in, _SLOT_WIDTH), np.float32)
    pk = np.zeros((n_in, _SLOT_WIDTH), np.float32)
    pv = np.zeros((FOX_WIDTH, _SLOT_WIDTH), np.float32)
    one_q = np.zeros((1, _SLOT_WIDTH), np.float32)
    one_k = np.zeros((1, _SLOT_WIDTH), np.float32)
    one_v = np.zeros((1, _SLOT_WIDTH), np.float32)
    for h in range(FOX_HEADS):
        for dd in range(FOX_HEAD_DIM):
            pq[h * FOX_HEAD_DIM + dd, h * _SLOT + dd] = 1.0
            pk[h * FOX_HEAD_DIM + dd, h * _SLOT + dd] = 1.0
            pv[h * FOX_HEAD_DIM + dd, h * _SLOT + dd] = 1.0
        for piece in range(3):
            pq[FOX_WIDTH + piece * LANES + h, h * _SLOT + _AUX + piece] = 1.0
            one_k[0, h * _SLOT + _AUX + piece] = 1.0
            pk[FOX_WIDTH + piece * LANES + h, h * _SLOT + _AUX + 3 + piece] = -1.0
            one_q[0, h * _SLOT + _AUX + 3 + piece] = 1.0
        one_v[0, h * _SLOT + _AUX] = 1.0
    as_bf = lambda a: jnp.asarray(a, BF16)
    return as_bf(pq), as_bf(pk), as_bf(pv), jnp.asarray(one_q), jnp.asarray(one_k), jnp.asarray(one_v)


def _fox_prep_kernel(q_ref, k_ref, v_ref, lf_ref, pq_ref, pk_ref, pv_ref, oq_ref, ok_ref, ov_ref,
                     qs_ref, ks_ref, vs_ref, carry_ref):
    @pl.when(pl.program_id(1) == 0)
    def _():
        carry_ref[...] = jnp.zeros(carry_ref.shape, F32)

    rows = q_ref.shape[0]
    lower = (_iota((LANES, LANES), 0) >= _iota((LANES, LANES), 1)).astype(BF16)
    for ci in range(rows // LANES):
        rs = slice(ci * LANES, (ci + 1) * LANES)
        c = _dot_f32_rhs(lower, lf_ref[rs, :]) + carry_ref[...]
        carry_ref[...] = c[LANES - 1:LANES, :]
        pieces = list(_split3(c * LOG2_E))
        qs_ref[rs, :] = (_dot(jnp.concatenate([q_ref[rs, :]] + pieces, axis=1), pq_ref[...])
                         + oq_ref[...]).astype(BF16)
        ks_ref[rs, :] = (_dot(jnp.concatenate([k_ref[rs, :]] + pieces, axis=1), pk_ref[...])
                         + ok_ref[...]).astype(BF16)
        vs_ref[rs, :] = (_dot(v_ref[rs, :], pv_ref[...]) + ov_ref[...]).astype(BF16)


def _fox_prep(fq, fkb, fvb, logf_wide, n, s_len):
    tm = min(_PREP_BLOCK, s_len)
    steps = s_len // tm
    pq, pk, pv, one_q, one_k, one_v = _slot_layout()
    tok = lambda w: pl.BlockSpec((tm, w), lambda b, t: (b * steps + t, 0))
    whole = lambda a: pl.BlockSpec(a.shape, lambda b, t: (0, 0))
    out = jax.ShapeDtypeStruct((n * s_len, _SLOT_WIDTH), BF16)
    return pl.pallas_call(
        _fox_prep_kernel,
        out_shape=(out, out, out),
        grid=(n, steps),
        in_specs=[tok(FOX_WIDTH), tok(FOX_WIDTH), tok(FOX_WIDTH), tok(LANES),
                  whole(pq), whole(pk), whole(pv), whole(one_q), whole(one_k), whole(one_v)],
        out_specs=(tok(_SLOT_WIDTH), tok(_SLOT_WIDTH), tok(_SLOT_WIDTH)),
        scratch_shapes=[pltpu.VMEM((1, LANES), F32)],
        compiler_params=_params(("parallel", "arbitrary")),
        name="fox_prep",
    )(fq, fkb, fvb, logf_wide, pq, pk, pv, one_q, one_k, one_v)


_FOX_BLOCK = 512


def _fox_prompt_kernel(qi_tab, ki_tab, q_ref, k_ref, v_ref, o_ref, m_ref, acc_ref):
    step = pl.program_id(1)
    qi = qi_tab[step]
    ki = ki_tab[step]
    tq, tk = q_ref.shape[0], k_ref.shape[0]

    @pl.when(ki == 0)
    def _():
        m_ref[...] = jnp.full(m_ref.shape, -1e30, F32)
        acc_ref[...] = jnp.zeros(acc_ref.shape, F32)

    def scores(h):
        hs = slice(h * _SLOT, (h + 1) * _SLOT)
        return _dot_nt(q_ref[:, hs], k_ref[:, hs])

    def sweep(diagonal):
        s_next = scores(0)
        for h in range(FOX_HEADS):
            hs = slice(h * _SLOT, (h + 1) * _SLOT)
            s = s_next
            if h + 1 < FOX_HEADS:
                s_next = scores(h + 1)
            if diagonal:
                s = jnp.where(_iota((tq, tk), 1) <= _iota((tq, tk), 0), s, -jnp.inf)
            m_prev = m_ref[h]
            m_new = jnp.maximum(m_prev, jnp.max(s, axis=1, keepdims=True))
            p = jnp.exp2(s - m_new)
            m_ref[h] = m_new
            acc_ref[h] = acc_ref[h] * jnp.exp2(m_prev - m_new) + _dot(p.astype(BF16), v_ref[:, hs])

    @pl.when(ki < qi)
    def _():
        sweep(False)

    @pl.when(ki == qi)
    def _():
        sweep(True)
        for h in range(FOX_HEADS):
            a = acc_ref[h]
            o_ref[:, h * _SLOT:(h + 1) * _SLOT] = (a / a[:, _AUX:_AUX + 1]).astype(o_ref.dtype)


def _fox_prompt(q_slots, k_slots, v_slots, n, s_len):
    tq = min(_FOX_BLOCK, s_len)
    nq = s_len // tq
    pairs = [(a, b) for a in range(nq) for b in range(a + 1)]
    qi_tab = jnp.asarray(np.array([p[0] for p in pairs], np.int32))
    ki_tab = jnp.asarray(np.array([p[1] for p in pairs], np.int32))
    grid_spec = pltpu.PrefetchScalarGridSpec(
        num_scalar_prefetch=2,
        grid=(n, len(pairs)),
        in_specs=[
            pl.BlockSpec((tq, _SLOT_WIDTH), lambda b, t, qt, kt: (b * nq + qt[t], 0)),
            pl.BlockSpec((tq, _SLOT_WIDTH), lambda b, t, qt, kt: (b * nq + kt[t], 0)),
            pl.BlockSpec((tq, _SLOT_WIDTH), lambda b, t, qt, kt: (b * nq + kt[t], 0)),
        ],
        out_specs=pl.BlockSpec((tq, _SLOT_WIDTH), lambda b, t, qt, kt: (b * nq + qt[t], 0)),
        scratch_shapes=[
            pltpu.VMEM((FOX_HEADS, tq, 1), F32),
            pltpu.VMEM((FOX_HEADS, tq, _SLOT), F32),
        ],
    )
    return pl.pallas_call(
        _fox_prompt_kernel,
        out_shape=jax.ShapeDtypeStruct((n * s_len, _SLOT_WIDTH), BF16),
        grid_spec=grid_spec,
        compiler_params=_params(("parallel", "arbitrary")),
        name="fox_prompt",
    )(qi_tab, ki_tab, q_slots, k_slots, v_slots)


def _rows_per_head(x, reps):
    return jnp.concatenate([jnp.broadcast_to(x[h:h + 1], (reps, x.shape[1])) for h in range(x.shape[0])], axis=0)


def _fox_sample_kernel(pt_ref, q_ref, knt_ref, vnt_ref, lfn_ref, *refs, n_new, page):
    del pt_ref
    pp = PAGES_PER_STEP
    k_refs, v_refs, lf_refs = refs[:pp], refs[pp:2 * pp], refs[2 * pp:3 * pp]
    o_ref, qbd_ref, carry_ref, m_ref, l_ref, acc_ref, kbf_ref, vbf_ref = refs[3 * pp:]
    step = pl.program_id(1)
    rows = FOX_HEADS * n_new

    def attend(s, vals_t):
        m_prev = m_ref[...]
        m_new = jnp.maximum(m_prev, jnp.max(s, axis=1, keepdims=True))
        p = jnp.exp2(s - m_new)
        corr = jnp.exp2(m_prev - m_new)
        l_ref[...] = corr * l_ref[...] + jnp.sum(p, axis=1, keepdims=True)
        m_ref[...] = m_new
        acc_ref[...] = acc_ref[...] * corr + _dot_nt(p.astype(BF16), vals_t)

    @pl.when(step == 0)
    def _():
        q = q_ref[...]
        q_rows = jnp.concatenate([q] * FOX_HEADS, axis=0)
        own = (_iota((rows, FOX_WIDTH), 0) // n_new) == (_iota((rows, FOX_WIDTH), 1) // FOX_HEAD_DIM)
        qbd = jnp.where(own, q_rows, jnp.zeros_like(q_rows))
        qbd_ref[...] = qbd
        carry_ref[...] = jnp.zeros(carry_ref.shape, F32)
        m_ref[...] = jnp.full(m_ref.shape, -1e30, F32)
        l_ref[...] = jnp.zeros(l_ref.shape, F32)
        acc_ref[...] = jnp.zeros(acc_ref.shape, F32)
        incl = (_iota((LANES, LANES), 0) <= _iota((LANES, LANES), 1)).astype(BF16)
        cn_rows = _rows_per_head(_dot_f32_lhs(lfn_ref[0], incl) * LOG2_E, n_new)
        s = _dot(qbd, knt_ref[0]) - cn_rows
        visible = _iota((rows, LANES), 1) <= (_iota((rows, LANES), 0) % n_new)
        attend(jnp.where(visible, s, -jnp.inf), vnt_ref[0])

    for i in range(pp):
        kbf_ref[:, i * page:(i + 1) * page] = k_refs[i][0].reshape(FOX_WIDTH, page).astype(BF16)
        vbf_ref[:, i * page:(i + 1) * page] = v_refs[i][0].reshape(FOX_WIDTH, page).astype(BF16)
    lf = jnp.concatenate([r[0] for r in lf_refs], axis=0)
    later = (_iota((page, page), 0) > _iota((page, page), 1)).astype(BF16)
    in_page = _dot_f32_lhs(lf, later)
    totals = jnp.sum(lf, axis=1, keepdims=True)
    carry = carry_ref[...]
    bias = []
    for i in range(pp):
        hs = slice(i * FOX_HEADS, (i + 1) * FOX_HEADS)
        bias.append(_rows_per_head((in_page[hs] + carry) * LOG2_E, n_new))
        carry = carry + totals[hs]
    carry_ref[...] = carry
    attend(_dot(qbd_ref[...], kbf_ref[...]) + jnp.concatenate(bias, axis=1), vbf_ref[...])

    @pl.when(step == pl.num_programs(1) - 1)
    def _():
        o = acc_ref[...] / l_ref[...]
        lane_head = _iota((n_new, FOX_WIDTH), 1) // FOX_HEAD_DIM
        out = jnp.zeros((n_new, FOX_WIDTH), F32)
        for h in range(FOX_HEADS):
            out = jnp.where(lane_head == h, o[h * n_new:(h + 1) * n_new], out)
        o_ref[...] = out.astype(o_ref.dtype)


def _fox_sample(fq, k_new_t, v_new_t, logf_new_t, cache_kt, cache_vt, cache_logf_t, page_table, n_new):
    b, n_pages = page_table.shape
    page = cache_kt.shape[3]
    pp = PAGES_PER_STEP
    rows = FOX_HEADS * n_new
    assert n_pages % pp == 0 and page == LANES

    def page_spec(shape, i):
        return pl.BlockSpec(shape, lambda s, t, pt: (pt[s, n_pages - 1 - (t * pp + i)],) + (0,) * (len(shape) - 1))

    seq = lambda shape: pl.BlockSpec(shape, lambda s, t, pt: (s, 0, 0))
    grid_spec = pltpu.PrefetchScalarGridSpec(
        num_scalar_prefetch=1,
        grid=(b, n_pages // pp),
        in_specs=([pl.BlockSpec((n_new, FOX_WIDTH), lambda s, t, pt: (s, 0)),
                   seq((1, FOX_WIDTH, LANES)), seq((1, FOX_WIDTH, LANES)), seq((1, FOX_HEADS, LANES))]
                  + [page_spec((1, FOX_HEADS, FOX_HEAD_DIM, page), i) for i in range(pp)]
                  + [page_spec((1, FOX_HEADS, FOX_HEAD_DIM, page), i) for i in range(pp)]
                  + [page_spec((1, FOX_HEADS, page), i) for i in range(pp)]),
        out_specs=pl.BlockSpec((n_new, FOX_WIDTH), lambda s, t, pt: (s, 0)),
        scratch_shapes=[
            pltpu.VMEM((rows, FOX_WIDTH), BF16),
            pltpu.VMEM((FOX_HEADS, 1), F32),
            pltpu.VMEM((rows, 1), F32),
            pltpu.VMEM((rows, 1), F32),
            pltpu.VMEM((rows, FOX_WIDTH), F32),
            pltpu.VMEM((FOX_WIDTH, pp * page), BF16),
            pltpu.VMEM((FOX_WIDTH, pp * page), BF16),
        ],
    )
    return pl.pallas_call(
        functools.partial(_fox_sample_kernel, n_new=n_new, page=page),
        out_shape=jax.ShapeDtypeStruct((b * n_new, FOX_WIDTH), BF16),
        grid_spec=grid_spec,
        compiler_params=_params(("parallel", "arbitrary")),
        name="fox_sample",
    )(page_table, fq, k_new_t, v_new_t, logf_new_t, *([cache_kt] * pp), *([cache_vt] * pp), *([cache_logf_t] * pp))


_GLA_CHUNKS_PER_STEP = 4


def _gla_kernel(q_ref, la_ref, v_ref, gg_ref, kt_ref, lat_ref, gn_ref, s0_ref, og_ref, sfin_ref, st_ref, *, cps):
    c = GLA_CHUNK
    step = pl.program_id(1)

    @pl.when(step == 0)
    def _():
        st_ref[...] = s0_ref[0]

    lower = (_iota((c, c), 0) >= _iota((c, c), 1))
    lower_b = lower.astype(BF16)
    upper_b = (_iota((c, c), 0) <= _iota((c, c), 1)).astype(BF16)
    for ci in range(cps):
        rs = slice(ci * c, (ci + 1) * c)
        for h in range(GLA_HEADS):
            ks = slice(h * GLA_DK, (h + 1) * GLA_DK)
            vs = slice(h * GLA_DV, (h + 1) * GLA_DV)
            b = _dot_f32_rhs(lower_b, la_ref[rs, ks])
            bt = _dot_f32_lhs(lat_ref[0, ks, rs], upper_b)
            b_last = bt[:, c - 1:c]
            kt = kt_ref[0, ks, rs]
            q_dec = (q_ref[rs, ks] * jnp.exp(b)).astype(BF16)
            k_dec_t = (kt * jnp.exp(-bt)).astype(BF16)
            k_end_t = (kt * jnp.exp(b_last - bt)).astype(BF16)
            v = v_ref[rs, vs]
            attn = jnp.where(lower, _dot(q_dec, k_dec_t), 0.0)
            state = st_ref[h]
            o = _dot(attn.astype(BF16), v) + _dot(q_dec, state.astype(BF16))
            st_ref[h] = state * jnp.exp(b_last) + _dot(k_end_t, v)
            scale = lax.rsqrt(jnp.mean(jnp.square(o), axis=1, keepdims=True) + RMS_EPS)
            og_ref[rs, vs] = (o * scale * gn_ref[:, vs] * gg_ref[rs, vs]).astype(og_ref.dtype)

    @pl.when(step == pl.num_programs(1) - 1)
    def _():
        sfin_ref[0] = st_ref[...]


def _gla(gq, la, gv, gg, gk_t, la_t, gla_norm, s0, n_seq, seq):
    cps = min(_GLA_CHUNKS_PER_STEP, seq // GLA_CHUNK)
    ts = cps * GLA_CHUNK
    steps = seq // ts
    tok = lambda s, t: (s * steps + t, 0)
    return pl.pallas_call(
        functools.partial(_gla_kernel, cps=cps),
        out_shape=(jax.ShapeDtypeStruct((n_seq * seq, GLA_V_WIDTH), BF16),
                   jax.ShapeDtypeStruct((n_seq, GLA_HEADS, GLA_DK, GLA_DV), F32)),
        grid=(n_seq, steps),
        in_specs=[
            pl.BlockSpec((ts, GLA_QK_WIDTH), tok),
            pl.BlockSpec((ts, GLA_QK_WIDTH), tok),
            pl.BlockSpec((ts, GLA_V_WIDTH), tok),
            pl.BlockSpec((ts, GLA_V_WIDTH), tok),
            pl.BlockSpec((1, GLA_QK_WIDTH, ts), lambda s, t: (s, 0, t)),
            pl.BlockSpec((1, GLA_QK_WIDTH, ts), lambda s, t: (s, 0, t)),
            pl.BlockSpec((1, GLA_V_WIDTH), lambda s, t: (0, 0)),
            pl.BlockSpec((1, GLA_HEADS, GLA_DK, GLA_DV), lambda s, t: (s, 0, 0, 0)),
        ],
        out_specs=(pl.BlockSpec((ts, GLA_V_WIDTH), tok),
                   pl.BlockSpec((1, GLA_HEADS, GLA_DK, GLA_DV), lambda s, t: (s, 0, 0, 0))),
        scratch_shapes=[pltpu.VMEM((GLA_HEADS, GLA_DK, GLA_DV), F32)],
        compiler_params=_params(("parallel", "arbitrary")),
        name="gla",
    )(gq, la, gv, gg, gk_t, la_t, gla_norm, s0)


_MIX_BLOCK = 512


def _pack_bf16_pairs(x):
    w = x.shape[1] // 2
    bits = lax.bitcast_convert_type(x.astype(BF16).astype(F32), jnp.uint32)
    return (bits[:, :w] >> 16) | (bits[:, w:] & jnp.uint32(0xFFFF0000))


def _unpack_bf16_pairs(u):
    lo = lax.bitcast_convert_type(u << 16, F32).astype(BF16)
    hi = lax.bitcast_convert_type(u & jnp.uint32(0xFFFF0000), F32).astype(BF16)
    return lo, hi


def _layer_norm(z, g, b):
    mu = jnp.mean(z, axis=1, keepdims=True)
    zc = z - mu
    var = jnp.mean(jnp.square(zc), axis=1, keepdims=True)
    return zc * lax.rsqrt(var + LN_EPS) * g + b


def _mix_kernel(of_ref, og_ref, ga_ref, gb_ref, x_ref, p_ref, wof_ref, wog_ref, wout_ref, wpg_ref, bpg_ref,
                wpp_ref, g1_ref, b1_ref, wr_ref, br_ref, base_ref, x1p_ref, lg_ref):
    branch_a = _dot(of_ref[...], wof_ref[...])
    branch_b = _dot(og_ref[...], wog_ref[...])
    merged = ga_ref[...] * branch_a + gb_ref[...] * branch_b
    mix = _dot(merged.astype(BF16), wout_ref[...])
    x1 = _layer_norm(DEEPNORM_ALPHA * x_ref[...] + mix, g1_ref[...], b1_ref[...])
    gate = _sigmoid(_dot(x1.astype(BF16), wpg_ref[...]) + bpg_ref[...])
    ple = gate * _dot(p_ref[...].astype(BF16), wpp_ref[...])
    base_ref[...] = DEEPNORM_ALPHA * x1 + ple
    x1p_ref[...] = _pack_bf16_pairs(x1)
    logits = jnp.dot(x1, wr_ref[...], precision=lax.Precision.HIGHEST, preferred_element_type=F32)
    lg_ref[...] = logits.T[:N_EXPERTS] + br_ref[...]


def _mix(o_fox, w_o_fox, og, gate_a, gate_b, x2d, p2d, wp):
    t, d = x2d.shape
    tm = min(_MIX_BLOCK, t)
    ple_dim = p2d.shape[1]
    fox_w = o_fox.shape[1]
    row = lambda w: pl.BlockSpec((tm, w), lambda i: (i, 0))
    whole = lambda shape: pl.BlockSpec(shape, lambda i: (0,) * len(shape))
    return pl.pallas_call(
        _mix_kernel,
        out_shape=(jax.ShapeDtypeStruct((t, d), F32),
                   jax.ShapeDtypeStruct((t, d // 2), jnp.uint32),
                   jax.ShapeDtypeStruct((N_EXPERTS, t), F32)),
        grid=(t // tm,),
        in_specs=[row(fox_w), row(GLA_V_WIDTH), row(d), row(d), row(d), row(ple_dim),
                  whole((fox_w, d)), whole((GLA_V_WIDTH, d)), whole((d, d)), whole((d, d)), whole((1, d)),
                  whole((ple_dim, d)), whole((1, d)), whole((1, d)), whole((d, LANES)), whole((N_EXPERTS, 1))],
        out_specs=(row(d), row(d // 2), pl.BlockSpec((N_EXPERTS, tm), lambda i: (0, i))),
        compiler_params=_params(("parallel",)),
        name="mix_ln_router",
    )(o_fox, og, gate_a, gate_b, x2d, p2d, w_o_fox, wp["w_o_gla"], wp["w_out"], wp["w_ple_gate"],
      wp["b_ple_gate"], wp["w_ple_proj"], wp["ln1_g"], wp["ln1_b"], wp["w_router"], wp["b_router"])


_ROUTE_BLOCK = 512


def _route_kernel(lg_ref, e_ref, w_ref, rank_ref, cnt_ref, carry_ref):
    i = pl.program_id(0)
    tn = lg_ref.shape[1]

    @pl.when(i == 0)
    def _():
        carry_ref[...] = jnp.zeros(carry_ref.shape, F32)

    lg = lg_ref[...]
    eidx = _iota((N_EXPERTS, tn), 0).astype(F32)
    vals, picks = [], []
    chosen_f = jnp.zeros((N_EXPERTS, tn), F32)
    for _ in range(TOP_K):
        mx = jnp.max(lg, axis=0, keepdims=True)
        idx = jnp.min(jnp.where(lg == mx, eidx, float(N_EXPERTS)), axis=0, keepdims=True)
        pick = eidx == idx
        vals.append(mx)
        picks.append(pick)
        chosen_f = jnp.where(pick, 1.0, chosen_f)
        lg = jnp.where(pick, -jnp.inf, lg)
    ex = [jnp.exp(v - vals[0]) for v in vals]
    den = ex[0] + ex[1] + ex[2] + ex[3]
    upper = (_iota((tn, tn), 0) <= _iota((tn, tn), 1)).astype(BF16)
    before = carry_ref[...] + _dot(chosen_f.astype(BF16), upper) - chosen_f
    for k in range(TOP_K):
        e_ref[k:k + 1, :] = jnp.sum(jnp.where(picks[k], eidx, 0.0), axis=0, keepdims=True).astype(jnp.int32)
        w_ref[k:k + 1, :] = ex[k] / den
        rank_ref[k:k + 1, :] = jnp.sum(jnp.where(picks[k], before, 0.0), axis=0, keepdims=True).astype(jnp.int32)
    carry_ref[...] = carry_ref[...] + jnp.sum(chosen_f, axis=1, keepdims=True)
    cnt_ref[...] = jnp.broadcast_to(carry_ref[...], cnt_ref.shape)


def _route(logits_t):
    t = logits_t.shape[1]
    tn = min(_ROUTE_BLOCK, t)
    blk = pl.BlockSpec((TOP_K, tn), lambda i: (0, i))
    return pl.pallas_call(
        _route_kernel,
        out_shape=(jax.ShapeDtypeStruct((TOP_K, t), jnp.int32),
                   jax.ShapeDtypeStruct((TOP_K, t), F32),
                   jax.ShapeDtypeStruct((TOP_K, t), jnp.int32),
                   jax.ShapeDtypeStruct((N_EXPERTS, LANES), F32)),
        grid=(t // tn,),
        in_specs=[pl.BlockSpec((N_EXPERTS, tn), lambda i: (0, i))],
        out_specs=(blk, blk, blk, pl.BlockSpec((N_EXPERTS, LANES), lambda i: (0, 0))),
        scratch_shapes=[pltpu.VMEM((N_EXPERTS, 1), F32)],
        compiler_params=_params(("arbitrary",)),
        name="moe_route",
    )(logits_t)


def _slots_kernel(e_ref, rank_ref, cnt_ref, dest_ref, be_ref, *, block_rows):
    tn = e_ref.shape[1]
    cnt = cnt_ref[...]
    padded = jnp.ceil(cnt * (1.0 / block_rows)) * block_rows
    incl = (_iota((N_EXPERTS, N_EXPERTS), 0) >= _iota((N_EXPERTS, N_EXPERTS), 1)).astype(BF16)
    ends = _dot_f32_rhs(incl, padded)
    starts = (ends - padded)[:, 0:1]
    eidx = _iota((N_EXPERTS, tn), 0)
    for k in range(TOP_K):
        first = jnp.sum(jnp.where(eidx == e_ref[k:k + 1, :], starts, 0.0), axis=0, keepdims=True)
        dest_ref[k:k + 1, :] = first.astype(jnp.int32) + rank_ref[k:k + 1, :]
    nb = be_ref.shape[1]
    row_start = (_iota((N_EXPERTS, nb), 1) * block_rows).astype(F32)
    owner = jnp.sum(jnp.where(ends[:, 0:1] <= row_start, 1.0, 0.0), axis=0, keepdims=True)
    be_ref[...] = jnp.minimum(owner, N_EXPERTS - 1.0).astype(jnp.int32)


def _slots(experts, ranks, counts, block_rows, n_blocks):
    t = experts.shape[1]
    tn = min(_ROUTE_BLOCK, t)
    nb_pad = -(-n_blocks // LANES) * LANES
    blk = pl.BlockSpec((TOP_K, tn), lambda i: (0, i))
    return pl.pallas_call(
        functools.partial(_slots_kernel, block_rows=block_rows),
        out_shape=(jax.ShapeDtypeStruct((TOP_K, t), jnp.int32),
                   jax.ShapeDtypeStruct((1, nb_pad), jnp.int32)),
        grid=(t // tn,),
        in_specs=[blk, blk, pl.BlockSpec((N_EXPERTS, LANES), lambda i: (0, 0))],
        out_specs=(blk, pl.BlockSpec((1, nb_pad), lambda i: (0, 0))),
        compiler_params=_params(("arbitrary",)),
        name="moe_slots",
    )(experts, ranks, counts)


_DISPATCH_BLOCK = 512


def _dispatch_kernel(dest_ref, x_ref, zero_ref, xb_ref, sem):
    del zero_ref
    tn = dest_ref.shape[1]

    def issue(j, carry):
        for k in range(TOP_K):
            pltpu.make_async_copy(x_ref.at[pl.ds(j, 1)], xb_ref.at[pl.ds(dest_ref[k, j], 1)], sem).start(priority=k % 2)
        return carry

    lax.fori_loop(0, tn, issue, 0)
    for k in range(TOP_K):
        pltpu.make_async_copy(x_ref, xb_ref.at[pl.ds(0, tn)], sem).wait()


def _dispatch(x_packed, dest, n_slots):
    t, w = x_packed.shape
    tn = min(_DISPATCH_BLOCK, t)
    zeros = jnp.zeros((n_slots, w), x_packed.dtype)
    return pl.pallas_call(
        _dispatch_kernel,
        out_shape=jax.ShapeDtypeStruct((n_slots, w), x_packed.dtype),
        grid=(t // tn,),
        in_specs=[pl.BlockSpec((TOP_K, tn), lambda i: (0, i), memory_space=pltpu.SMEM),
                  pl.BlockSpec((tn, w), lambda i: (i, 0)),
                  pl.BlockSpec(memory_space=pl.ANY)],
        out_specs=pl.BlockSpec(memory_space=pl.ANY),
        scratch_shapes=[pltpu.SemaphoreType.DMA(())],
        input_output_aliases={2: 0},
        compiler_params=_params(("arbitrary",)),
        name="moe_dispatch",
    )(dest, x_packed, zeros)


def _expert_kernel(be_ref, x_ref, wgu_ref, bgu_ref, wd_ref, bd_ref, y_ref):
    del be_ref
    lo, hi = _unpack_bf16_pairs(x_ref[...])
    half = lo.shape[1]
    h = _dot(lo, wgu_ref[0, :half]) + _dot(hi, wgu_ref[0, half:]) + bgu_ref[0]
    d_ff = h.shape[1] // 2
    gate = jnp.minimum(h[:, :d_ff], SWIGLU_LIMIT)
    up = jnp.clip(h[:, d_ff:], -SWIGLU_LIMIT, SWIGLU_LIMIT)
    hid = (up + 1.0) * gate * _sigmoid(SWIGLU_ALPHA * gate)
    y_ref[...] = _dot(hid.astype(BF16), wd_ref[0]) + bd_ref[0]


def _experts(xb, block_expert, w_gu, b_gu, w_down, b_down, block_rows, n_blocks):
    d = w_down.shape[2]
    wexp = lambda *shape: pl.BlockSpec((1,) + shape, lambda i, be: (be[i],) + (0,) * len(shape))
    grid_spec = pltpu.PrefetchScalarGridSpec(
        num_scalar_prefetch=1,
        grid=(n_blocks,),
        in_specs=[pl.BlockSpec((block_rows, xb.shape[1]), lambda i, be: (i, 0)),
                  wexp(*w_gu.shape[1:]), wexp(*b_gu.shape[1:]), wexp(*w_down.shape[1:]), wexp(*b_down.shape[1:])],
        out_specs=pl.BlockSpec((block_rows, d), lambda i, be: (i, 0)),
    )
    return pl.pallas_call(
        _expert_kernel,
        out_shape=jax.ShapeDtypeStruct((n_blocks * block_rows, d), F32),
        grid_spec=grid_spec,
        compiler_params=_params(("parallel",)),
        name="moe_experts",
    )(block_expert, xb, w_gu, b_gu, w_down, b_down)


_COMBINE_BLOCK = 256


def _combine_kernel(dest_ref, yb_ref, w_ref, base_ref, g_ref, b_ref, o_ref, buf_ref, sem):
    tm = base_ref.shape[0]

    def issue(j, carry):
        for k in range(TOP_K):
            pltpu.make_async_copy(yb_ref.at[pl.ds(dest_ref[k, j], 1)], buf_ref.at[k, pl.ds(j, 1)],
                                  sem).start(priority=k % 2)
        return carry

    lax.fori_loop(0, tm, issue, 0)
    for k in range(TOP_K):
        pltpu.make_async_copy(yb_ref.at[pl.ds(0, tm)], buf_ref.at[k], sem).wait()
    y = base_ref[...]
    for k in range(TOP_K):
        y = y + buf_ref[k] * w_ref[:, k:k + 1]
    o_ref[...] = _layer_norm(y, g_ref[...], b_ref[...])


def _combine(yb, dest, w_tok, base, ln_g, ln_b):
    t, d = base.shape
    tm = min(_COMBINE_BLOCK, t)
    return pl.pallas_call(
        _combine_kernel,
        out_shape=jax.ShapeDtypeStruct((t, d), F32),
        grid=(t // tm,),
        in_specs=[pl.BlockSpec((TOP_K, tm), lambda i: (0, i), memory_space=pltpu.SMEM),
                  pl.BlockSpec(memory_space=pl.ANY),
                  pl.BlockSpec((tm, TOP_K), lambda i: (i, 0)),
                  pl.BlockSpec((tm, d), lambda i: (i, 0)),
                  pl.BlockSpec((1, d), lambda i: (0, 0)),
                  pl.BlockSpec((1, d), lambda i: (0, 0))],
        out_specs=pl.BlockSpec((tm, d), lambda i: (i, 0)),
        scratch_shapes=[pltpu.VMEM((TOP_K, tm, d), F32), pltpu.SemaphoreType.DMA(())],
        compiler_params=_params(("arbitrary",)),
        name="moe_combine",
    )(dest, yb, w_tok, base, ln_g, ln_b)


def _prepare_weights(w_in, b_f, w_a_up, b_a, gla_norm, w_o_fox, w_o_gla, w_out, ln1_g, ln1_b, w_router, b_router,
                     w_gu, b_gu, w_down, b_down, w_ple_gate, b_ple_gate, w_ple_proj, ln2_g, ln2_b):
    d = w_in.shape[0]
    widths = (FOX_WIDTH, FOX_WIDTH, FOX_WIDTH, FOX_HEADS, GLA_QK_WIDTH, GLA_QK_WIDTH, GLA_V_WIDTH, GLA_V_WIDTH,
              GLA_GATE_RANK, d, d)
    offs = np.cumsum((0,) + widths)
    fq, fk, fv, ff, gq, gk, gv, gg, glr, ga, gb = [w_in[:, offs[i]:offs[i + 1]] for i in range(len(widths))]

    def pad_cols(w, n):
        return jnp.pad(w, ((0, 0), (0, n - w.shape[1])))

    row = lambda v: v.reshape(1, -1).astype(F32)
    w_au_pad = jnp.pad(w_a_up, ((0, LANES - GLA_GATE_RANK), (0, 0)))
    return {
        "w_main": jnp.concatenate([fq, fk, fv, gq, gv, gg, ga, gb], axis=1).astype(BF16),
        "w_ff": pad_cols(ff, LANES).astype(BF16),
        "b_f": pad_cols(row(b_f), LANES),
        "w_lr": pad_cols(glr, LANES).astype(BF16),
        "w_au": w_au_pad.astype(BF16),
        "b_a": row(b_a),
        "w_gk_t": gk.T.astype(BF16),
        "w_lr_t": pad_cols(glr, LANES).T.astype(BF16),
        "w_au_t": w_au_pad.T.astype(BF16),
        "b_a_t": b_a.reshape(-1, 1).astype(F32),
        "gla_norm": row(gla_norm),
        "w_o_fox": w_o_fox.astype(BF16),
        "w_o_fox_slots": jnp.pad(w_o_fox.reshape(FOX_HEADS, FOX_HEAD_DIM, d),
                                 ((0, 0), (0, _SLOT - FOX_HEAD_DIM), (0, 0))).reshape(_SLOT_WIDTH, d).astype(BF16),
        "w_o_gla": w_o_gla.astype(BF16), "w_out": w_out.astype(BF16),
        "ln1_g": row(ln1_g), "ln1_b": row(ln1_b),
        "w_router": pad_cols(w_router, LANES).astype(F32), "b_router": b_router.reshape(-1, 1).astype(F32),
        "w_gu": w_gu.astype(BF16), "b_gu": b_gu.reshape(N_EXPERTS, 1, -1).astype(F32),
        "w_down": w_down.astype(BF16), "b_down": b_down.reshape(N_EXPERTS, 1, -1).astype(F32),
        "w_ple_gate": w_ple_gate.astype(BF16), "b_ple_gate": row(b_ple_gate),
        "w_ple_proj": w_ple_proj.astype(BF16),
        "ln2_g": row(ln2_g), "ln2_b": row(ln2_b),
    }


def _moe_and_norm(base, x1_packed, logits_t, wp, block_rows):
    t = base.shape[0]
    n_blocks = -(-(t * TOP_K) // block_rows) + N_EXPERTS
    experts, weights, ranks, counts = _route(logits_t)
    dest, block_expert = _slots(experts, ranks, counts, block_rows, n_blocks)
    xb = _dispatch(x1_packed, dest, n_blocks * block_rows)
    yb = _experts(xb, block_expert[0, :n_blocks], wp["w_gu"], wp["b_gu"], wp["w_down"], wp["b_down"],
                  block_rows, n_blocks)
    return _combine(yb, dest, weights.T, base, wp["ln2_g"], wp["ln2_b"])


def _prompt_layer(x, p, wp):
    n, s_len, d = x.shape
    x2d = x.reshape(n * s_len, d)
    fq, fk, fv, fkb, fvb, gq, gv, gg, ga, gb, logf, logf_wide, la, gk_t, la_t = _in_projection(x2d, wp, n)
    q_slots, k_slots, v_slots = _fox_prep(fq, fkb, fvb, logf_wide, n, s_len)
    o_fox = _fox_prompt(q_slots, k_slots, v_slots, n, s_len)
    s0 = jnp.zeros((n, GLA_HEADS, GLA_DK, GLA_DV), F32)
    og, s_fin = _gla(gq, la, gv, gg, gk_t, la_t, wp["gla_norm"], s0, n, s_len)
    base, x1p, logits_t = _mix(o_fox, wp["w_o_fox_slots"], og, ga, gb, x2d, p.reshape(n * s_len, -1), wp)
    y = _moe_and_norm(base, x1p, logits_t, wp, block_rows=256)
    return (y.reshape(n, s_len, d), fk.reshape(1, n, s_len, FOX_HEADS, FOX_HEAD_DIM),
            fv.reshape(1, n, s_len, FOX_HEADS, FOX_HEAD_DIM), logf.reshape(1, n, s_len, FOX_HEADS), s_fin[None])


def _sample_layer(x, p, cache_k, cache_v, cache_logf, state, page_table, wp):
    b, n_new, d = x.shape
    t = b * n_new
    x2d = x.reshape(t, d)
    fq, fk, fv, fkb, fvb, gq, gv, gg, ga, gb, logf, _, la, gk_t, la_t = _in_projection(x2d, wp, 1)
    new_t = lambda a: jnp.pad(a.reshape(b, n_new, -1).transpose(0, 2, 1), ((0, 0), (0, 0), (0, LANES - n_new)))
    o_fox = _fox_sample(fq, new_t(fkb), new_t(fvb), new_t(logf), cache_k.transpose(0, 2, 3, 1),
                        cache_v.transpose(0, 2, 3, 1), cache_logf.transpose(0, 2, 1), page_table, n_new)
    c = GLA_CHUNK
    pad_tok = lambda a: jnp.pad(a.reshape(b, n_new, -1), ((0, 0), (0, c - n_new), (0, 0))).reshape(b * c, -1)
    pad_time = lambda a: jnp.pad(a.reshape(-1, b, n_new).transpose(1, 0, 2), ((0, 0), (0, 0), (0, c - n_new)))
    og, s_fin = _gla(pad_tok(gq), pad_tok(la), pad_tok(gv), pad_tok(gg), pad_time(gk_t), pad_time(la_t),
                     wp["gla_norm"], state, b, c)
    og = og.reshape(b, c, -1)[:, :n_new].reshape(t, -1)
    base, x1p, logits_t = _mix(o_fox, wp["w_o_fox"], og, ga, gb, x2d, p.reshape(t, -1), wp)
    y = _moe_and_norm(base, x1p, logits_t, wp, block_rows=128)
    return (y.reshape(b, n_new, d), fk.reshape(1, b, n_new, FOX_HEADS, FOX_HEAD_DIM),
            fv.reshape(1, b, n_new, FOX_HEADS, FOX_HEAD_DIM), logf.reshape(1, b, n_new, FOX_HEADS), s_fin[None])


def kernel(x_prompt, x_sample, cache_k, cache_v, cache_logf, state_gla, page_table, p_prompt, p_sample, w_in, b_f,
           w_a_up, b_a, gla_norm, w_o_fox, w_o_gla, w_out, ln1_g, ln1_b, w_router, b_router, w_gu, b_gu, w_down,
           b_down, w_ple_gate, b_ple_gate, w_ple_proj, ln2_g, ln2_b):
    assert w_in.shape[0] == DEPTH
    wp = _prepare_weights(w_in[0], b_f[0], w_a_up[0], b_a[0], gla_norm[0], w_o_fox[0], w_o_gla[0], w_out[0],
                          ln1_g[0], ln1_b[0], w_router[0], b_router[0], w_gu[0], b_gu[0], w_down[0], b_down[0],
                          w_ple_gate[0], b_ple_gate[0], w_ple_proj[0], ln2_g[0], ln2_b[0])
    y_p, k_p, v_p, lf_p, s_p = _prompt_layer(x_prompt, p_prompt[0], wp)
    y_s, k_s, v_s, lf_s, s_s = _sample_layer(x_sample, p_sample[0], cache_k[0], cache_v[0], cache_logf[0],
                                             state_gla[0], page_table, wp)
    return (y_p, y_s, k_p, v_p, lf_p, s_p, k_s, v_s, lf_s, s_s)
```
